```python
import math
import jax, jax.numpy as jnp
from jax import lax
import numpy as np

D_MODEL = 1024
BATCH = 8
SEQ = 4096
DEPTH = 4

GRID_W = 64
CTX_LEN = 256
HEAD_DIM = 64
A_Q_HEADS = 8
A_KV_HEADS = 2
B_HEADS = 4
B_KEY_DIM = 128
B_VAL_DIM = 128
HGRN_CHUNK = 32
C_HEADS = 16
NA_ROWS = 8
NA_COLS = 16
D_FF = 2816
CONV_W = 3

ROPE_THETA = 10000.0
Q_BLOCK = 128
EPS = 1e-6
ATTN_SCALE = HEAD_DIM ** -0.5
HGRN_SCALE = B_KEY_DIM ** -0.5

N_EVEN = (DEPTH + 1) // 2
N_ODD = DEPTH // 2
A_Q_W = A_Q_HEADS * HEAD_DIM
A_KV_W = A_KV_HEADS * HEAD_DIM
B_K_W = B_HEADS * B_KEY_DIM
B_V_W = B_HEADS * B_VAL_DIM
EVEN_IN_W = A_Q_W + 2 * A_KV_W + 3 * B_K_W + 2 * B_V_W
EVEN_SPLITS = (A_Q_W, A_Q_W + A_KV_W, A_Q_W + 2 * A_KV_W, A_Q_W + 2 * A_KV_W + B_K_W,
               A_Q_W + 2 * A_KV_W + 2 * B_K_W, A_Q_W + 2 * A_KV_W + 3 * B_K_W,
               A_Q_W + 2 * A_KV_W + 3 * B_K_W + B_V_W)
EVEN_OUT_W = A_Q_W + B_V_W
C_W = C_HEADS * HEAD_DIM

kernel_name = "hybrid_gqa_hgrn2_natten_dit_prefix"


def rms_norm(x, w):
    xf = x.astype(jnp.float32)
    y = xf * lax.rsqrt(jnp.mean(xf * xf, axis=-1, keepdims=True) + EPS)
    return (y * w.astype(jnp.float32)).astype(x.dtype)


def axial_rope_tables(n_tokens):
    t = jnp.arange(n_tokens)
    row = (t // GRID_W).astype(jnp.float32)
    col = (t % GRID_W).astype(jnp.float32)
    n_freq = HEAD_DIM // 4
    inv_freq = ROPE_THETA ** (-jnp.arange(n_freq, dtype=jnp.float32) / n_freq)
    ang = jnp.concatenate([row[:, None] * inv_freq, col[:, None] * inv_freq], axis=-1)
    return jnp.cos(ang), jnp.sin(ang)


def apply_rope(x, cos, sin):
    x1, x2 = jnp.split(x, 2, axis=-1)
    c = cos[None, :, None, :].astype(x.dtype)
    s = sin[None, :, None, :].astype(x.dtype)
    return jnp.concatenate([x1 * c - x2 * s, x1 * s + x2 * c], axis=-1)


def gqa_axial_attention(q_lat, k_lat, v_lat, q_ctx, k_ctx, v_ctx, q_norm_w, k_norm_w, with_ctx_out):
    B_, L = q_lat.shape[:2]
    n_ctx = q_ctx.shape[1]
    group = A_Q_HEADS // A_KV_HEADS
    q_lat, q_ctx = rms_norm(q_lat, q_norm_w), rms_norm(q_ctx, q_norm_w)
    k_lat, k_ctx = rms_norm(k_lat, k_norm_w), rms_norm(k_ctx, k_norm_w)
    cos, sin = axial_rope_tables(L)
    q_lat = apply_rope(q_lat, cos, sin)
    k_lat = apply_rope(k_lat, cos, sin)
    k_all = jnp.concatenate([k_ctx, k_lat], axis=1)
    v_all = jnp.concatenate([v_ctx, v_lat], axis=1)

    def attend(q, k, v):
        s = jnp.einsum("bqhgd,bkhd->bhgqk", q, k).astype(jnp.float32) * ATTN_SCALE
        p = jax.nn.softmax(s, axis=-1).astype(v.dtype)
        return jnp.einsum("bhgqk,bkhd->bqhgd", p, v)

    n_blk = L // Q_BLOCK
    q_blocks = q_lat.reshape(B_, n_blk, Q_BLOCK, A_KV_HEADS, group, HEAD_DIM).transpose(1, 0, 2, 3, 4, 5)
    o = lax.map(lambda qb: attend(qb, k_all, v_all), q_blocks)
    o_lat = o.transpose(1, 0, 2, 3, 4, 5).reshape(B_, L, A_Q_W)
    o_ctx = None
    if with_ctx_out:
        o_ctx = attend(q_ctx.reshape(B_, n_ctx, A_KV_HEADS, group, HEAD_DIM), k_ctx, v_ctx).reshape(B_, n_ctx, A_Q_W)
    return o_lat, o_ctx


def forget_gate(f_raw, lb):
    z = f_raw.astype(jnp.float32)
    log_f = jnp.logaddexp(jnp.log(lb), jnp.log1p(-lb) + jax.nn.log_sigmoid(z))
    k = (1.0 - lb) * jax.nn.sigmoid(-z)
    return k, log_f


def hgrn2_chunk_scan(q, k, log_f, v, s0):
    B_, L, H, dk = q.shape
    dv = v.shape[-1]
    n = L // HGRN_CHUNK

    def to_chunks(a):
        return a.reshape(B_, n, HGRN_CHUNK, H, a.shape[-1]).transpose(1, 0, 3, 2, 4)

    tri = jnp.tril(jnp.ones((HGRN_CHUNK, HGRN_CHUNK), dtype=bool))[None, None, :, :, None]

    def step(S, inp):
        qc, kc, gc, vc = inp
        b = jnp.cumsum(gc, axis=2)
        diff = b[:, :, :, None, :] - b[:, :, None, :, :]
        decay = jnp.exp(jnp.where(tri, diff, -jnp.inf))
        scores = jnp.einsum("bhtd,bhtsd,bhsd->bhts", qc, decay, kc)
        o = jnp.einsum("bhts,bhsv->bhtv", scores, vc) + jnp.einsum("bhtd,bhdv->bhtv", qc * jnp.exp(b), S)
        b_last = b[:, :, -1:, :]
        S_new = jnp.exp(b_last[:, :, 0, :])[..., None] * S + jnp.einsum("bhsd,bhsv->bhdv", kc * jnp.exp(b_last - b), vc)
        return S_new, o

    S, o = lax.scan(step, s0, (to_chunks(q), to_chunks(k), to_chunks(log_f), to_chunks(v)))
    o = o.transpose(1, 0, 3, 2, 4).reshape(B_, L, H, dv).astype(v.dtype)
    return o, S


def hgrn2_bidirectional(q_lat, f_lat, i_lat, q_ctx, f_ctx, i_ctx, lb):
    B_ = q_lat.shape[0]
    s0 = jnp.zeros((B_, B_HEADS, B_KEY_DIM, B_VAL_DIM), jnp.float32)
    o_lat, o_ctx = None, None
    for d in range(2):
        flip = (lambda a: a) if d == 0 else (lambda a: a[:, ::-1])
        k_l, lf_l = forget_gate(f_lat[d], lb[d])
        k_c, lf_c = forget_gate(f_ctx[d], lb[d])
        oc, s_ctx = hgrn2_chunk_scan(flip(q_ctx), flip(k_c), flip(lf_c), flip(i_ctx), s0)
        ol, _ = hgrn2_chunk_scan(flip(q_lat), flip(k_l), flip(lf_l), flip(i_lat), s_ctx)
        o_lat = flip(ol) if o_lat is None else o_lat + flip(ol)
        o_ctx = flip(oc) if o_ctx is None else o_ctx + flip(oc)
    return o_lat, o_ctx


def even_mixer(h_lat, h_ctx, w_in, w_out, q_norm_w, k_norm_w, lb, out_norm_w, with_ctx_out):
    def project(h):
        B_, n = h.shape[:2]
        aq, ak, av, bq, bff, bfb, bi, bg = jnp.split(h @ w_in, EVEN_SPLITS, axis=-1)
        bshape = (B_, n, B_HEADS, B_KEY_DIM)
        return (aq.reshape(B_, n, A_Q_HEADS, HEAD_DIM), ak.reshape(B_, n, A_KV_HEADS, HEAD_DIM),
                av.reshape(B_, n, A_KV_HEADS, HEAD_DIM), (jax.nn.silu(bq) * HGRN_SCALE).reshape(bshape),
                (bff.reshape(bshape), bfb.reshape(bshape)), bi.reshape(B_, n, B_HEADS, B_VAL_DIM), bg)

    aq_l, ak_l, av_l, bq_l, bf_l, bi_l, bg_l = project(h_lat)
    aq_c, ak_c, av_c, bq_c, bf_c, bi_c, bg_c = project(h_ctx)
    oa_lat, oa_ctx = gqa_axial_attention(aq_l, ak_l, av_l, aq_c, ak_c, av_c, q_norm_w, k_norm_w, with_ctx_out)
    ob_lat, ob_ctx = hgrn2_bidirectional(bq_l, bf_l, bi_l, bq_c, bf_c, bi_c, lb)

    def readout(o, g):
        B_, n = o.shape[:2]
        return (rms_norm(o, out_norm_w).reshape(B_, n, B_V_W) * jax.nn.silu(g)).astype(g.dtype)

    y_lat = jnp.concatenate([oa_lat, readout(ob_lat, bg_l)], axis=-1) @ w_out
    y_ctx = None
    if with_ctx_out:
        y_ctx = jnp.concatenate([oa_ctx, readout(ob_ctx, bg_c)], axis=-1) @ w_out
    return y_lat, y_ctx


def odd_mixer(h_lat, h_ctx, w_qkv, w_out, rpb, with_ctx_out):
    B_, L, _ = h_lat.shape
    n_ctx = h_ctx.shape[1]
    rows = L // GRID_W
    wr = min(NA_ROWS, rows)
    qkv = (h_lat @ w_qkv).reshape(B_, rows, GRID_W, 3, C_HEADS, HEAD_DIM)
    q, k, v = qkv[:, :, :, 0], qkv[:, :, :, 1], qkv[:, :, :, 2]
    qkv_c = (h_ctx @ w_qkv).reshape(B_, n_ctx, 3, C_HEADS, HEAD_DIM)
    q_c, k_c, v_c = qkv_c[:, :, 0], qkv_c[:, :, 1], qkv_c[:, :, 2]

    r_start = jnp.clip(jnp.arange(rows) - wr // 2, 0, rows - wr)
    c_start = jnp.clip(jnp.arange(GRID_W) - NA_COLS // 2, 0, GRID_W - NA_COLS)
    c_idx = c_start[:, None] + jnp.arange(NA_COLS)
    dx = c_idx - jnp.arange(GRID_W)[:, None] + (NA_COLS - 1)

    def row_block(inp):
        r, q_r = inp
        r0 = r_start[r]
        k_band = lax.dynamic_slice_in_dim(k, r0, wr, axis=1)
        v_band = lax.dynamic_slice_in_dim(v, r0, wr, axis=1)
        k_win = k_band[:, :, c_idx]
        v_win = v_band[:, :, c_idx]
        dy = r0 + jnp.arange(wr) - r + (NA_ROWS - 1)
        bias = rpb[:, dy[:, None, None], dx[None, :, :]].transpose(0, 2, 1, 3)
        s_loc = jnp.einsum("bqhd,brqjhd->bhqrj", q_r, k_win).astype(jnp.float32) * ATTN_SCALE
        s_loc = (s_loc + bias[None].astype(jnp.float32)).reshape(B_, C_HEADS, GRID_W, wr * NA_COLS)
        s_ctx = jnp.einsum("bqhd,bchd->bhqc", q_r, k_c).astype(jnp.float32) * ATTN_SCALE
        p = jax.nn.softmax(jnp.concatenate([s_loc, s_ctx], axis=-1), axis=-1).astype(v.dtype)
        p_loc = p[..., :wr * NA_COLS].reshape(B_, C_HEADS, GRID_W, wr, NA_COLS)
        p_ctx = p[..., wr * NA_COLS:]
        return jnp.einsum("bhqrj,brqjhd->bqhd", p_loc, v_win) + jnp.einsum("bhqc,bchd->bqhd", p_ctx, v_c)

    o = lax.map(row_block, (jnp.arange(rows), q.transpose(1, 0, 2, 3, 4)))
    y_lat = o.transpose(1, 0, 2, 3, 4).reshape(B_, L, C_W) @ w_out
    y_ctx = None
    if with_ctx_out:
        s = jnp.einsum("bqhd,bkhd->bhqk", q_c, k_c).astype(jnp.float32) * ATTN_SCALE
        p = jax.nn.softmax(s, axis=-1).astype(v_c.dtype)
        y_ctx = jnp.einsum("bhqk,bkhd->bqhd", p, v_c).reshape(B_, n_ctx, C_W) @ w_out
    return y_lat, y_ctx


def conv_ffn(h, w_up, conv_w, conv_b, w_down):
    u = h @ w_up
    u = lax.conv_general_dilated(u, conv_w[:, None, :], window_strides=(1,),
                                 padding=[(CONV_W // 2, CONV_W // 2)],
                                 dimension_numbers=("NWC", "WIO", "NWC"),
                                 feature_group_count=u.shape[-1]) + conv_b
    a, g = jnp.split(u, 2, axis=-1)
    return (jax.nn.silu(g) * a) @ w_down


def setup_inputs(seed: int = 0) -> dict:
    key = jax.random.key(seed)
    ks = jax.random.split(key, 24)

    def nrm(k, shape, s):
        return jax.random.normal(k, shape, jnp.float32) * s

    return {
        "x": nrm(ks[0], (BATCH, SEQ, D_MODEL), 1.0),
        "c": nrm(ks[1], (BATCH, D_MODEL), 1.0),
        "ctx": nrm(ks[2], (BATCH, CTX_LEN, D_MODEL), 1.0),
        "c_ctx": nrm(ks[3], (D_MODEL,), 1.0),
        "w_mod": nrm(ks[4], (DEPTH, D_MODEL, 6 * D_MODEL), D_MODEL ** -0.5),
        "b_mod": nrm(ks[5], (DEPTH, 6 * D_MODEL), 0.02),
        "norm_pre_mix": 1.0 + nrm(ks[6], (DEPTH, D_MODEL), 0.1),
        "norm_post_mix": 1.0 + nrm(ks[7], (DEPTH, D_MODEL), 0.1),
        "norm_pre_ffn": 1.0 + nrm(ks[8], (DEPTH, D_MODEL), 0.1),
        "norm_post_ffn": 1.0 + nrm(ks[9], (DEPTH, D_MODEL), 0.1),
        "even_w_in": nrm(ks[10], (N_EVEN, D_MODEL, EVEN_IN_W), D_MODEL ** -0.5),
        "even_w_out": nrm(ks[11], (N_EVEN, EVEN_OUT_W, D_MODEL), EVEN_OUT_W ** -0.5),
        "even_q_norm": 1.0 + nrm(ks[12], (N_EVEN, HEAD_DIM), 0.1),
        "even_k_norm": 1.0 + nrm(ks[13], (N_EVEN, HEAD_DIM), 0.1),
        "hgrn_lb_logits": nrm(ks[14], (N_EVEN, 2, B_K_W), 1.0),
        "hgrn_out_norm": 1.0 + nrm(ks[15], (N_EVEN, B_VAL_DIM), 0.1),
        "odd_w_qkv": nrm(ks[16], (N_ODD, D_MODEL, 3 * C_W), D_MODEL ** -0.5),
        "odd_w_out": nrm(ks[17], (N_ODD, C_W, D_MODEL), C_W ** -0.5),
        "odd_rpb": nrm(ks[18], (N_ODD, C_HEADS, 2 * NA_ROWS - 1, 2 * NA_COLS - 1), 0.1),
        "ffn_w_up": nrm(ks[19], (DEPTH, D_MODEL, 2 * D_FF), D_MODEL ** -0.5),
        "ffn_conv_w": nrm(ks[20], (DEPTH, CONV_W, 2 * D_FF), CONV_W ** -0.5),
        "ffn_conv_b": nrm(ks[21], (DEPTH, 2 * D_FF), 0.02),
        "ffn_w_down": nrm(ks[22], (DEPTH, D_FF, D_MODEL), D_FF ** -0.5),
    }


def reference(x, c, ctx, c_ctx, w_mod, b_mod, norm_pre_mix, norm_post_mix, norm_pre_ffn, norm_post_ffn,
              even_w_in, even_w_out, even_q_norm, even_k_norm, hgrn_lb_logits, hgrn_out_norm,
              odd_w_qkv, odd_w_out, odd_rpb, ffn_w_up, ffn_conv_w, ffn_conv_b, ffn_w_down):
    p_lb = jax.nn.softmax(hgrn_lb_logits.astype(jnp.float32), axis=0)
    lb_all = jnp.concatenate([jnp.zeros_like(p_lb[:1]), jnp.cumsum(p_lb[1:], axis=0)], axis=0)
    lb_all = lb_all.reshape(N_EVEN, 2, B_HEADS, B_KEY_DIM)

    for l in range(DEPTH):
        with_ctx_out = l < DEPTH - 1
        mod_lat = (jax.nn.silu(c) @ w_mod[l] + b_mod[l])[:, None, :]
        mod_ctx = (jax.nn.silu(c_ctx) @ w_mod[l] + b_mod[l])[None, None, :]
        sh1, sc1, g1, sh2, sc2, g2 = jnp.split(mod_lat, 6, axis=-1)
        sh1c, sc1c, g1c, sh2c, sc2c, g2c = jnp.split(mod_ctx, 6, axis=-1)

        h_lat = rms_norm(x, norm_pre_mix[l]) * (1.0 + sc1) + sh1
        h_ctx = rms_norm(ctx, norm_pre_mix[l]) * (1.0 + sc1c) + sh1c
        if l % 2 == 0:
            e = l // 2
            y_lat, y_ctx = even_mixer(h_lat, h_ctx, even_w_in[e], even_w_out[e], even_q_norm[e], even_k_norm[e],
                                      lb_all[e], hgrn_out_norm[e], with_ctx_out)
        else:
            o = l // 2
            y_lat, y_ctx = odd_mixer(h_lat, h_ctx, odd_w_qkv[o], odd_w_out[o], odd_rpb[o], with_ctx_out)
        x = x + g1 * rms_norm(y_lat, norm_post_mix[l])

        h_lat = rms_norm(x, norm_pre_ffn[l]) * (1.0 + sc2) + sh2
        x = x + g2 * rms_norm(conv_ffn(h_lat, ffn_w_up[l], ffn_conv_w[l], ffn_conv_b[l], ffn_w_down[l]), norm_post_ffn[l])
        if with_ctx_out:
            ctx = ctx + g1c * rms_norm(y_ctx, norm_post_mix[l])
            h_ctx = rms_norm(ctx, norm_pre_ffn[l]) * (1.0 + sc2c) + sh2c
            ctx = ctx + g2c * rms_norm(conv_ffn(h_ctx, ffn_w_up[l], ffn_conv_w[l], ffn_conv_b[l], ffn_w_down[l]), norm_post_ffn[l])
    return x
```

```python
import functools

import jax
import jax.numpy as jnp
from jax import lax
from jax.experimental import pallas as pl
from jax.experimental.pallas import tpu as pltpu

F32 = jnp.float32
BF16 = jnp.bfloat16
HIGHEST = lax.Precision.HIGHEST

V7X_VMEM_BYTES = 64 * 1024 * 1024
VMEM_LIMIT_BYTES = V7X_VMEM_BYTES - 8 * 1024 * 1024
LANES = 128
BF16_SUBLANES = 16

GRID_W = 64
HEAD_DIM = 64
A_Q_HEADS = 8
A_KV_HEADS = 2
A_GROUP = A_Q_HEADS // A_KV_HEADS
B_HEADS = 4
B_KEY_DIM = 128
C_HEADS = 16
NA_ROWS = 8
NA_COLS = 16
EPS = 1e-6
ROPE_THETA = 10000.0
ATTN_SCALE = HEAD_DIM ** -0.5
HGRN_SCALE = B_KEY_DIM ** -0.5
NEG_BIG = -1e30

A_Q_W = A_Q_HEADS * HEAD_DIM
A_KV_W = A_KV_HEADS * HEAD_DIM
B_W = B_HEADS * B_KEY_DIM

HGRN_CHUNK = 64
HGRN_SUB = 16
HGRN_BLOCK = 256
KV_BLOCK = 256


def _cparams(*sem):
    return pltpu.CompilerParams(dimension_semantics=sem, vmem_limit_bytes=VMEM_LIMIT_BYTES)


def _const_spec(shape):
    n = len(shape)
    return pl.BlockSpec(shape, lambda *_: (0,) * n)


def _silu(x):
    return x / (1.0 + jnp.exp(-x))


def _rms(x, w):
    return x * lax.rsqrt(jnp.mean(x * x, axis=-1, keepdims=True) + EPS) * w


def _norm_mod(x, w, scale, shift):
    return _rms(x, w) * (1.0 + scale) + shift


def _mod_body(c_ref, w_ref, b_ref, o_ref):
    s = _silu(c_ref[...])
    o_ref[0] = jnp.dot(s, w_ref[0], preferred_element_type=F32, precision=HIGHEST) + b_ref[0]


def _modulation(c, c_ctx, w_mod, b_mod):
    depth, d, n = w_mod.shape
    b = c.shape[0]
    rows = -(-(b + 1) // 8) * 8
    cc = jnp.concatenate([c, c_ctx[None, :], jnp.zeros((rows - b - 1, d), F32)], axis=0)
    tn = n // 4
    out = pl.pallas_call(
        _mod_body,
        grid=(depth, n // tn),
        in_specs=[
            pl.BlockSpec((rows, d), lambda l, j: (0, 0)),
            pl.BlockSpec((1, d, tn), lambda l, j: (l, 0, j)),
            pl.BlockSpec((1, 1, tn), lambda l, j: (l, 0, j)),
        ],
        out_specs=pl.BlockSpec((1, rows, tn), lambda l, j: (l, 0, j)),
        out_shape=jax.ShapeDtypeStruct((depth, rows, n), F32),
        compiler_params=_cparams("parallel", "parallel"),
        name="adaln_mod",
    )(cc, w_mod, b_mod.reshape(depth, 1, n))
    return out.reshape(depth, rows, 6, d)


def _mod_index(seq_len, tm, mod_row):
    if mod_row is None:
        per_seq = seq_len // tm
        return lambda i: (i // per_seq, 0, 0)
    return lambda i: (mod_row, 0, 0)


def _even_in_body(x_ref, mod_ref, nw_ref, w_ref, qw_ref, kw_ref, cos_ref, sin_ref, gmat_ref,
                  qa_ref, ka_ref, va_ref, bq_ref, z_ref, bi_ref, sg_ref, *, use_rope):
    h = _norm_mod(x_ref[...], nw_ref[...], mod_ref[0, 1:2, :], mod_ref[0, 0:1, :]).astype(BF16)
    tm = h.shape[0]
    lane = lax.broadcasted_iota(jnp.int32, (tm, LANES), 1)
    lo_half = (lane % HEAD_DIM) < (HEAD_DIM // 2)
    lo_head = lane < HEAD_DIM
    gmat = gmat_ref[...]

    def proj(a, b):
        return jnp.dot(h, w_ref[:, a:b], preferred_element_type=F32)

    def head_norm_rope(zc, w):
        ms = jnp.dot(zc * zc, gmat, preferred_element_type=F32, precision=HIGHEST)
        y = zc * lax.rsqrt(ms + EPS) * w
        if use_rope:
            rot = jnp.where(lo_half, pltpu.roll(y, LANES - HEAD_DIM // 2, 1), pltpu.roll(y, HEAD_DIM // 2, 1))
            y = y * cos_ref[...] + rot * sin_ref[...]
        return y

    zq = proj(0, A_Q_W)
    for c in range(A_Q_W // LANES):
        y = head_norm_rope(zq[:, c * LANES:(c + 1) * LANES], qw_ref[...]) * ATTN_SCALE
        swapped = pltpu.roll(y, HEAD_DIM, 1)
        g = (2 * c) // A_GROUP
        if g == 0:
            h0, h1 = jnp.where(lo_head, y, 0.0), jnp.where(lo_head, swapped, 0.0)
        else:
            h0, h1 = jnp.where(lo_head, 0.0, swapped), jnp.where(lo_head, 0.0, y)
        qa_ref[:, (2 * c) * LANES:(2 * c + 1) * LANES] = h0.astype(BF16)
        qa_ref[:, (2 * c + 1) * LANES:(2 * c + 2) * LANES] = h1.astype(BF16)
    o = A_Q_W
    ka_ref[...] = head_norm_rope(proj(o, o + A_KV_W), kw_ref[...]).astype(BF16)
    o += A_KV_W
    va_ref[...] = proj(o, o + A_KV_W).astype(BF16)
    o += A_KV_W
    bq_ref[...] = (_silu(proj(o, o + B_W)) * HGRN_SCALE).astype(BF16)
    o += B_W
    z_ref[...] = proj(o, o + 2 * B_W)
    o += 2 * B_W
    bi_ref[...] = proj(o, o + B_W).astype(BF16)
    o += B_W
    sg_ref[...] = _silu(proj(o, o + B_W)).astype(BF16)


def _even_in(x, modtab, norm_w, w_in, q_norm, k_norm, rope, gmat, *, seq_len, mod_row, tm):
    n, d = x.shape
    use_rope = rope is not None
    cos, sin = rope if use_rope else (jnp.zeros((8, LANES), F32),) * 2
    per_seq = seq_len // tm
    rope_spec = pl.BlockSpec((tm, LANES), lambda i: (i % per_seq, 0)) if use_rope else _const_spec((8, LANES))
    row = lambda w: pl.BlockSpec((tm, w), lambda i: (i, 0))
    widths = (2 * A_Q_W, A_KV_W, A_KV_W, B_W, 2 * B_W, B_W, B_W)
    dtypes = (BF16, BF16, BF16, BF16, F32, BF16, BF16)
    return pl.pallas_call(
        functools.partial(_even_in_body, use_rope=use_rope),
        grid=(n // tm,),
        in_specs=[
            row(d),
            pl.BlockSpec((1, 6, d), _mod_index(seq_len, tm, mod_row)),
            _const_spec((1, d)),
            _const_spec(w_in.shape),
            _const_spec((1, LANES)),
            _const_spec((1, LANES)),
            rope_spec, rope_spec,
            _const_spec((LANES, LANES)),
        ],
        out_specs=[row(w) for w in widths],
        out_shape=[jax.ShapeDtypeStruct((n, w), t) for w, t in zip(widths, dtypes)],
        compiler_params=_cparams("parallel"),
        name="even_in",
    )(x, modtab, norm_w.reshape(1, d), w_in, q_norm, k_norm, cos, sin, gmat)


def _gqa_body(*refs, tq, n_lat_blk):
    if n_lat_blk:
        q_ref, kc_ref, vc_ref, kl_ref, vl_ref, o_ref, m_ref, l_ref, acc_ref = refs
    else:
        q_ref, kc_ref, vc_ref, o_ref, m_ref, l_ref, acc_ref = refs
    lo_head = lax.broadcasted_iota(jnp.int32, (tq, LANES), 1) < HEAD_DIM
    for g in range(A_KV_HEADS):
        q = jnp.concatenate(
            [q_ref[0, :, (A_GROUP * g + j) * LANES:(A_GROUP * g + j + 1) * LANES] for j in range(A_GROUP)], axis=0)
        m_ref[...] = jnp.full(m_ref.shape, -jnp.inf, F32)
        l_ref[...] = jnp.zeros(l_ref.shape, F32)
        acc_ref[...] = jnp.zeros(acc_ref.shape, F32)

        def step(k, v):
            s = lax.dot_general(q, k, (((1,), (1,)), ((), ())), preferred_element_type=F32)
            m_prev = m_ref[...]
            m_new = jnp.maximum(m_prev, jnp.max(s, axis=1, keepdims=True))
            alpha = jnp.exp(m_prev - m_new)
            p = jnp.exp(s - jnp.concatenate([m_new] * (s.shape[1] // LANES), axis=1))
            l_ref[...] = alpha * l_ref[...] + jnp.sum(p, axis=1, keepdims=True)
            acc_ref[...] = alpha * acc_ref[...] + jnp.dot(p.astype(BF16), v, preferred_element_type=F32)
            m_ref[...] = m_new

        step(kc_ref[0], vc_ref[0])
        if n_lat_blk:
            def lat_step(j, carry):
                start = pl.multiple_of(j * KV_BLOCK, KV_BLOCK)
                step(kl_ref[0, pl.ds(start, KV_BLOCK), :], vl_ref[0, pl.ds(start, KV_BLOCK), :])
                return carry
            lax.fori_loop(0, n_lat_blk, lat_step, 0)

        o = acc_ref[...] / l_ref[...]
        for c in range(A_GROUP // 2):
            o0 = o[(2 * c) * tq:(2 * c + 1) * tq]
            o1 = o[(2 * c + 1) * tq:(2 * c + 2) * tq]
            if g == 0:
                pair = jnp.where(lo_head, o0, pltpu.roll(o1, HEAD_DIM, 1))
            else:
                pair = jnp.where(lo_head, pltpu.roll(o0, HEAD_DIM, 1), o1)
            col = (A_GROUP // 2 * g + c) * LANES
            o_ref[0, :, col:col + LANES] = pair.astype(BF16)


def _gqa(q, k_ctx, v_ctx, k_lat=None, v_lat=None, *, tq):
    b, nq, _ = q.shape
    n_ctx = k_ctx.shape[1]
    assert n_ctx == KV_BLOCK
    with_lat = k_lat is not None
    n_lat = k_lat.shape[1] if with_lat else 0
    kv_spec = lambda n: pl.BlockSpec((1, n, A_KV_W), lambda i, j: (i, 0, 0))
    in_specs = [pl.BlockSpec((1, tq, 2 * A_Q_W), lambda i, j: (i, j, 0)), kv_spec(n_ctx), kv_spec(n_ctx)]
    args = [q, k_ctx, v_ctx]
    if with_lat:
        in_specs += [kv_spec(n_lat), kv_spec(n_lat)]
        args += [k_lat, v_lat]
    rows = A_GROUP * tq
    return pl.pallas_call(
        functools.partial(_gqa_body, tq=tq, n_lat_blk=n_lat // KV_BLOCK),
        grid=(b, nq // tq),
        in_specs=in_specs,
        out_specs=pl.BlockSpec((1, tq, A_Q_W), lambda i, j: (i, j, 0)),
        out_shape=jax.ShapeDtypeStruct((b, nq, A_Q_W), BF16),
        scratch_shapes=[pltpu.VMEM((rows, LANES), F32)] * 3,
        compiler_params=_cparams("parallel", "parallel"),
        name="gqa_lat" if with_lat else "gqa_ctx",
    )(*args)


def _hgrn_masks(rev):
    c = HGRN_CHUNK
    t = lax.broadcasted_iota(jnp.int32, (c, c), 0)
    s = lax.broadcasted_iota(jnp.int32, (c, c), 1)
    causal = (s >= t) if rev else (s <= t)
    diag = ((t // HGRN_SUB) == (s // HGRN_SUB)) & causal
    levels = []
    size = 2 * HGRN_SUB
    while size <= c:
        half = size // 2
        t_hi, s_hi = (t % size) >= half, (s % size) >= half
        pair = (~t_hi & s_hi) if rev else (t_hi & ~s_hi)
        levels.append((size, ((t // size) == (s // size)) & pair))
        size *= 2
    return diag, levels


def _hgrn_chunk(row0, rev, q_ref, z_ref, v_ref, lb_ref, tri_ref, sel_ref, o_ref, st_ref, kbuf, bbuf, pcat):
    c, sub = HGRN_CHUNK, HGRN_SUB
    rows = pl.ds(row0, c)
    diag_mask, level_masks = _hgrn_masks(rev)
    tri = tri_ref[...]
    per_head = []
    for hd in range(B_HEADS):
        cols = slice(hd * LANES, (hd + 1) * LANES)
        q = q_ref[0, rows, cols].astype(F32)
        z = z_ref[0, rows, cols]
        log_lb, log_1m_lb, one_m_lb = lb_ref[0, 0:1, cols], lb_ref[0, 1:2, cols], lb_ref[0, 2:3, cols]
        kg = one_m_lb / (1.0 + jnp.exp(z))
        cterm = log_1m_lb + (jnp.minimum(z, 0.0) - jnp.log1p(jnp.exp(-jnp.abs(z))))
        log_f = jnp.maximum(log_lb, cterm) + jnp.log1p(jnp.exp(-jnp.abs(log_lb - cterm)))
        b = jnp.dot(tri, log_f, preferred_element_type=F32, precision=HIGHEST)
        kbuf[hd] = kg
        bbuf[hd] = b
        for u in range(sub):
            ku = jnp.concatenate(
                [jnp.broadcast_to(kbuf[hd, m * sub + u:m * sub + u + 1, :], (sub, LANES)) for m in range(c // sub)], 0)
            bu = jnp.concatenate(
                [jnp.broadcast_to(bbuf[hd, m * sub + u:m * sub + u + 1, :], (sub, LANES)) for m in range(c // sub)], 0)
            pcat[hd * c:(hd + 1) * c, u * LANES:(u + 1) * LANES] = (
                q * ku * jnp.exp(jnp.minimum(b - bu, 0.0))).astype(BF16)
        per_head.append((q, kg, b))
    a_diag = jnp.dot(pcat[...], sel_ref[...], preferred_element_type=F32)
    for hd in range(B_HEADS):
        cols = slice(hd * LANES, (hd + 1) * LANES)
        q, kg, b = per_head[hd]
        v = v_ref[0, rows, cols]
        a = jnp.where(diag_mask, a_diag[hd * c:(hd + 1) * c], 0.0)
        for size, mask in level_masks:
            half = size // 2
            ref_rows = [m * size + (half if rev else half - 1) for m in range(c // size)]
            beta = jnp.concatenate(
                [jnp.broadcast_to(bbuf[hd, r:r + 1, :], (size, LANES)) for r in ref_rows], 0)
            e = jnp.exp(-jnp.abs(b - beta))
            lvl = lax.dot_general((q * e).astype(BF16), (kg * e).astype(BF16), (((1,), (1,)), ((), ())),
                                  preferred_element_type=F32)
            a = a + jnp.where(mask, lvl, 0.0)
        b_end = bbuf[hd, 0:1, :] if rev else bbuf[hd, c - 1:c, :]
        st = st_ref[0, 0, hd]
        o = jnp.dot(a.astype(BF16), v, preferred_element_type=F32)
        o = o + lax.dot_general((q * jnp.exp(b)).astype(BF16), st.astype(BF16), (((1,), (1,)), ((), ())),
                                preferred_element_type=F32)
        o_ref[0, 0, rows, cols] = o
        kd = (kg * jnp.exp(b_end - b)).astype(BF16)
        upd = lax.dot_general(v, kd, (((0,), (0,)), ((), ())), preferred_element_type=F32)
        st_ref[0, 0, hd] = st * jnp.exp(b_end) + upd


def _hgrn_body(q_ref, z_ref, v_ref, lb_ref, tri_ref, sel_ref, s0_ref, o_ref, st_ref, kbuf, bbuf, pcat):
    d, j = pl.program_id(1), pl.program_id(2)
    n_chunks = HGRN_BLOCK // HGRN_CHUNK

    @pl.when(j == 0)
    def _():
        st_ref[...] = s0_ref[...]

    for rev in (False, True):
        @pl.when(d == int(rev))
        def _(rev=rev):
            def chunk(i, carry):
                ci = (n_chunks - 1 - i) if rev else i
                row0 = pl.multiple_of(ci * HGRN_CHUNK, HGRN_CHUNK)
                _hgrn_chunk(row0, rev, q_ref, z_ref, v_ref, lb_ref, tri_ref, sel_ref, o_ref, st_ref,
                            kbuf, bbuf, pcat)
                return carry
            lax.fori_loop(0, n_chunks, chunk, 0)


def _hgrn(q, z, v, lbtab, tri, sel, s0):
    b, n, _ = q.shape
    nblk = n // HGRN_BLOCK
    c = HGRN_CHUNK

    def blk(d, j):
        return jnp.where(d == 0, j, nblk - 1 - j)

    tok = lambda i, d, j: (i, blk(d, j), 0)
    st_spec = pl.BlockSpec((1, 1, B_HEADS, LANES, LANES), lambda i, d, j: (i, d, 0, 0, 0))
    return pl.pallas_call(
        _hgrn_body,
        grid=(b, 2, nblk),
        in_specs=[
            pl.BlockSpec((1, HGRN_BLOCK, B_W), tok),
            pl.BlockSpec((1, HGRN_BLOCK, B_W), lambda i, d, j: (i, blk(d, j), d)),
            pl.BlockSpec((1, HGRN_BLOCK, B_W), tok),
            pl.BlockSpec((1, 8, B_W), lambda i, d, j: (d, 0, 0)),
            pl.BlockSpec((None, c, c), lambda i, d, j: (d, 0, 0)),
            _const_spec(sel.shape),
            st_spec,
        ],
        out_specs=[
            pl.BlockSpec((1, 1, HGRN_BLOCK, B_W), lambda i, d, j: (d, i, blk(d, j), 0)),
            st_spec,
        ],
        out_shape=[jax.ShapeDtypeStruct((2, b, n, B_W), F32),
                   jax.ShapeDtypeStruct(s0.shape, F32)],
        scratch_shapes=[pltpu.VMEM((B_HEADS, c, LANES), F32), pltpu.VMEM((B_HEADS, c, LANES), F32),
                        pltpu.VMEM((B_HEADS * c, HGRN_SUB * LANES), BF16)],
        compiler_params=_cparams("parallel", "parallel", "arbitrary"),
        name="hgrn_scan",
    )(q, z, v, lbtab, tri, sel, s0)


def _odd_in_body(x_ref, mod_ref, nw_ref, w_ref, q_ref, k_ref, v_ref):
    h = _norm_mod(x_ref[...], nw_ref[...], mod_ref[0, 1:2, :], mod_ref[0, 0:1, :]).astype(BF16)
    cw = q_ref.shape[1]
    q_ref[...] = (jnp.dot(h, w_ref[:, 0:cw], preferred_element_type=F32) * ATTN_SCALE).astype(BF16)
    k_ref[...] = jnp.dot(h, w_ref[:, cw:2 * cw], preferred_element_type=F32).astype(BF16)
    v_ref[...] = jnp.dot(h, w_ref[:, 2 * cw:3 * cw], preferred_element_type=F32).astype(BF16)


def _odd_in(x, modtab, norm_w, w_qkv, *, seq_len, mod_row, tm):
    n, d = x.shape
    cw = w_qkv.shape[1] // 3
    row = pl.BlockSpec((tm, cw), lambda i: (i, 0))
    return pl.pallas_call(
        _odd_in_body,
        grid=(n // tm,),
        in_specs=[
            pl.BlockSpec((tm, d), lambda i: (i, 0)),
            pl.BlockSpec((1, 6, d), _mod_index(seq_len, tm, mod_row)),
            _const_spec((1, d)),
            _const_spec(w_qkv.shape),
        ],
        out_specs=[row, row, row],
        out_shape=[jax.ShapeDtypeStruct((n, cw), BF16)] * 3,
        compiler_params=_cparams("parallel"),
        name="odd_in",
    )(x, modtab, norm_w.reshape(1, d), w_qkv)


def _na_body(ql_ref, kl_ref, vl_ref, qc_ref, kc_ref, vc_ref, bias_ref, ol_ref, oc_ref, *, rows):
    band = NA_ROWS * GRID_W
    lane_q = lax.broadcasted_iota(jnp.int32, (GRID_W, LANES), 1) < HEAD_DIM
    kc, vc = kc_ref[0], vc_ref[0]
    nt = (((1,), (1,)), ((), ()))

    def two_heads(q, lo_mask):
        zero = jnp.zeros_like(q)
        return jnp.concatenate([jnp.where(lo_mask, q, zero), jnp.where(lo_mask, zero, q)], axis=0)

    def merge(o, n, lo_mask):
        return jnp.where(lo_mask, o[:n], o[n:])

    def row_step(r, carry):
        r0 = jnp.clip(r - NA_ROWS // 2, 0, rows - NA_ROWS)
        q = two_heads(ql_ref[0, pl.ds(pl.multiple_of(r * GRID_W, GRID_W), GRID_W), :], lane_q)
        kstart = pl.multiple_of(r0 * GRID_W, GRID_W)
        kb = kl_ref[0, pl.ds(kstart, band), :]
        vb = vl_ref[0, pl.ds(kstart, band), :]
        s_loc = lax.dot_general(q, kb, nt, preferred_element_type=F32) + bias_ref[0, r - r0]
        s_ctx = lax.dot_general(q, kc, nt, preferred_element_type=F32)
        m = jnp.maximum(jnp.max(s_loc, axis=1, keepdims=True), jnp.max(s_ctx, axis=1, keepdims=True))
        p_loc = jnp.exp(s_loc - m)
        p_ctx = jnp.exp(s_ctx - m)
        denom = jnp.sum(p_loc, axis=1, keepdims=True) + jnp.sum(p_ctx, axis=1, keepdims=True)
        o = (jnp.dot(p_loc.astype(BF16), vb, preferred_element_type=F32)
             + jnp.dot(p_ctx.astype(BF16), vc, preferred_element_type=F32)) / denom
        ol_ref[0, pl.ds(pl.multiple_of(r * GRID_W, GRID_W), GRID_W), :] = merge(o, GRID_W, lane_q).astype(BF16)
        return carry

    lax.fori_loop(0, rows, row_step, 0)

    n_ctx = qc_ref.shape[1]
    lane_c = lax.broadcasted_iota(jnp.int32, (n_ctx, LANES), 1) < HEAD_DIM
    q = two_heads(qc_ref[0], lane_c)
    s = lax.dot_general(q, kc, nt, preferred_element_type=F32)
    p = jnp.exp(s - jnp.max(s, axis=1, keepdims=True))
    o = jnp.dot(p.astype(BF16), vc, preferred_element_type=F32) / jnp.sum(p, axis=1, keepdims=True)
    oc_ref[0] = merge(o, n_ctx, lane_c).astype(BF16)


def _na_bias(rpb):
    heads = rpb.shape[0]
    qc = jnp.arange(GRID_W)
    c_start = jnp.clip(qc - NA_COLS // 2, 0, GRID_W - NA_COLS)
    kc = jnp.arange(GRID_W)
    in_win = (kc[None, :] >= c_start[:, None]) & (kc[None, :] < c_start[:, None] + NA_COLS)
    dx = jnp.clip(kc[None, :] - qc[:, None] + (NA_COLS - 1), 0, 2 * NA_COLS - 2)
    var = jnp.arange(NA_ROWS)
    rr = jnp.arange(NA_ROWS)
    dy = rr[None, :] - var[:, None] + (NA_ROWS - 1)
    tab = rpb[:, dy[:, :, None, None], dx[None, None, :, :]]
    tab = jnp.where(in_win[None, None, None], tab, NEG_BIG)
    tab = tab.transpose(0, 1, 3, 2, 4).reshape(heads // 2, 2, NA_ROWS, GRID_W, NA_ROWS * GRID_W)
    return tab.transpose(0, 2, 1, 3, 4).reshape(heads // 2, NA_ROWS, 2 * GRID_W, NA_ROWS * GRID_W)


def _na(q_lat, k_lat, v_lat, q_ctx, k_ctx, v_ctx, bias):
    b, n, w = q_lat.shape
    n_ctx = q_ctx.shape[1]
    rows = n // GRID_W
    assert rows >= NA_ROWS
    pairs = w // LANES
    lat = pl.BlockSpec((1, n, LANES), lambda p, i: (i, 0, p))
    ctx = pl.BlockSpec((1, n_ctx, LANES), lambda p, i: (i, 0, p))
    return pl.pallas_call(
        functools.partial(_na_body, rows=rows),
        grid=(pairs, b),
        in_specs=[lat, lat, lat, ctx, ctx, ctx,
                  pl.BlockSpec((1,) + bias.shape[1:], lambda p, i: (p, 0, 0, 0))],
        out_specs=[lat, ctx],
        out_shape=[jax.ShapeDtypeStruct((b, n, w), BF16), jax.ShapeDtypeStruct((b, n_ctx, w), BF16)],
        compiler_params=_cparams("parallel", "parallel"),
        name="na_attn",
    )(q_lat, k_lat, v_lat, q_ctx, k_ctx, v_ctx, bias)


def _post(acc, post_w, gate, x):
    return x + gate * _rms(acc, post_w)


def _even_out_body(x_ref, mod_ref, oa_ref, ob_ref, sg_ref, onw_ref, w_ref, pw_ref, y_ref):
    acc = jnp.dot(oa_ref[...], w_ref[0:A_Q_W, :], preferred_element_type=F32)
    ob = ob_ref[0] + ob_ref[1]
    parts = []
    for hd in range(B_HEADS):
        cols = slice(hd * LANES, (hd + 1) * LANES)
        parts.append((_rms(ob[:, cols], onw_ref[...]) * sg_ref[:, cols].astype(F32)).astype(BF16))
    acc = acc + jnp.dot(jnp.concatenate(parts, axis=1), w_ref[A_Q_W:, :], preferred_element_type=F32)
    y_ref[...] = _post(acc, pw_ref[...], mod_ref[0, 2:3, :], x_ref[...])


def _even_out(x, modtab, oa, ob, sg, out_norm, w_out, post_w, *, seq_len, mod_row, tm):
    n, d = x.shape
    row = lambda w: pl.BlockSpec((tm, w), lambda i: (i, 0))
    return pl.pallas_call(
        _even_out_body,
        grid=(n // tm,),
        in_specs=[
            row(d),
            pl.BlockSpec((1, 6, d), _mod_index(seq_len, tm, mod_row)),
            row(A_Q_W),
            pl.BlockSpec((2, tm, B_W), lambda i: (0, i, 0)),
            row(B_W),
            _const_spec((1, LANES)),
            _const_spec(w_out.shape),
            _const_spec((1, d)),
        ],
        out_specs=row(d),
        out_shape=jax.ShapeDtypeStruct((n, d), F32),
        compiler_params=_cparams("parallel"),
        name="even_out",
    )(x, modtab, oa, ob, sg, out_norm.reshape(1, LANES), w_out, post_w.reshape(1, d))


def _odd_out_body(x_ref, mod_ref, o_ref, w_ref, pw_ref, y_ref):
    acc = jnp.dot(o_ref[...], w_ref[...], preferred_element_type=F32)
    y_ref[...] = _post(acc, pw_ref[...], mod_ref[0, 2:3, :], x_ref[...])


def _odd_out(x, modtab, o, w_out, post_w, *, seq_len, mod_row, tm):
    n, d = x.shape
    row = lambda w: pl.BlockSpec((tm, w), lambda i: (i, 0))
    return pl.pallas_call(
        _odd_out_body,
        grid=(n // tm,),
        in_specs=[
            row(d),
            pl.BlockSpec((1, 6, d), _mod_index(seq_len, tm, mod_row)),
            row(o.shape[1]),
            _const_spec(w_out.shape),
            _const_spec((1, d)),
        ],
        out_specs=row(d),
        out_shape=jax.ShapeDtypeStruct((n, d), F32),
        compiler_params=_cparams("parallel"),
        name="odd_out",
    )(x, modtab, o, w_out, post_w.reshape(1, d))


FFN_HALO = BF16_SUBLANES
FFN_TN = 256


def _ffn_body(x_ref, xp_ref, xn_ref, mod_ref, nw_ref, wup_ref, cw_ref, cb_ref, wdn_ref, pw_ref, y_ref,
              hbuf, ua, ug, acc, *, seq_len):
    tm, d = x_ref.shape
    dff = wdn_ref.shape[0]
    halo = FFN_HALO
    scale, shift = mod_ref[0, 4:5, :], mod_ref[0, 3:4, :]
    x = x_ref[...]
    hbuf[0:halo, :] = _norm_mod(xp_ref[...], nw_ref[...], scale, shift).astype(BF16)
    hbuf[halo:halo + tm, :] = _norm_mod(x, nw_ref[...], scale, shift).astype(BF16)
    hbuf[halo + tm:, :] = _norm_mod(xn_ref[...], nw_ref[...], scale, shift).astype(BF16)
    pos = (pl.program_id(0) * tm + lax.broadcasted_iota(jnp.int32, (tm, FFN_TN), 0)) % seq_len
    has_prev, has_next = pos != 0, pos != seq_len - 1
    acc[...] = jnp.zeros(acc.shape, F32)
    h = hbuf[...]
    for j in range(dff // FFN_TN):
        halves = []
        for buf, base in ((ua, 0), (ug, dff)):
            cols = slice(base + j * FFN_TN, base + (j + 1) * FFN_TN)
            buf[...] = jnp.dot(h, wup_ref[:, cols], preferred_element_type=F32)
            halves.append(
                cw_ref[0:1, cols] * jnp.where(has_prev, buf[halo - 1:halo - 1 + tm, :], 0.0)
                + cw_ref[1:2, cols] * buf[halo:halo + tm, :]
                + cw_ref[2:3, cols] * jnp.where(has_next, buf[halo + 1:halo + 1 + tm, :], 0.0)
                + cb_ref[0:1, cols])
        a, g = halves
        act = (_silu(g) * a).astype(BF16)
        acc[...] += jnp.dot(act, wdn_ref[j * FFN_TN:(j + 1) * FFN_TN, :], preferred_element_type=F32)
    y_ref[...] = _post(acc[...], pw_ref[...], mod_ref[0, 5:6, :], x)


def _ffn(x, modtab, norm_w, w_up, conv_w, conv_b, w_down, post_w, *, seq_len, mod_row, tm):
    n, d = x.shape
    dff = w_down.shape[0]
    assert dff % FFN_TN == 0 and tm % FFN_HALO == 0
    per_tile = tm // FFN_HALO
    last = n // FFN_HALO - 1
    return pl.pallas_call(
        functools.partial(_ffn_body, seq_len=seq_len),
        grid=(n // tm,),
        in_specs=[
            pl.BlockSpec((tm, d), lambda i: (i, 0)),
            pl.BlockSpec((FFN_HALO, d), lambda i: (jnp.maximum(i * per_tile - 1, 0), 0)),
            pl.BlockSpec((FFN_HALO, d), lambda i: (jnp.minimum((i + 1) * per_tile, last), 0)),
            pl.BlockSpec((1, 6, d), _mod_index(seq_len, tm, mod_row)),
            _const_spec((1, d)),
            _const_spec(w_up.shape),
            _const_spec(conv_w.shape),
            _const_spec((1, 2 * dff)),
            _const_spec(w_down.shape),
            _const_spec((1, d)),
        ],
        out_specs=pl.BlockSpec((tm, d), lambda i: (i, 0)),
        out_shape=jax.ShapeDtypeStruct((n, d), F32),
        scratch_shapes=[pltpu.VMEM((tm + 2 * FFN_HALO, d), BF16),
                        pltpu.VMEM((tm + 2 * FFN_HALO, FFN_TN), F32),
                        pltpu.VMEM((tm + 2 * FFN_HALO, FFN_TN), F32),
                        pltpu.VMEM((tm, d), F32)],
        compiler_params=_cparams("parallel"),
        name="conv_ffn",
    )(x, x, x, modtab, norm_w.reshape(1, d), w_up, conv_w, conv_b.reshape(1, 2 * dff), w_down,
      post_w.reshape(1, d))


def _rope_tables(n_tokens):
    t = jnp.arange(n_tokens)
    row = (t // GRID_W).astype(F32)
    col = (t % GRID_W).astype(F32)
    n_freq = HEAD_DIM // 4
    inv_freq = ROPE_THETA ** (-jnp.arange(n_freq, dtype=F32) / n_freq)
    ang = jnp.concatenate([row[:, None] * inv_freq, col[:, None] * inv_freq], axis=-1)
    cos, sin = jnp.cos(ang), jnp.sin(ang)
    cos_t = jnp.tile(cos, (1, LANES // (HEAD_DIM // 2)))
    sin_t = jnp.tile(jnp.concatenate([-sin, sin], axis=-1), (1, LANES // HEAD_DIM))
    return cos_t, sin_t


def _hgrn_tables():
    c = HGRN_CHUNK
    lower = jnp.tril(jnp.ones((c, c), F32))
    tri = jnp.stack([lower, lower.T])
    u = jnp.repeat(jnp.arange(HGRN_SUB), LANES)
    sel = (u[:, None] == (jnp.arange(c) % HGRN_SUB)[None, :]).astype(BF16)
    return tri, sel


def _lb_table(lb):
    pad = jnp.zeros((2, 5, lb.shape[-1]), F32)
    return jnp.concatenate([jnp.log(lb)[:, None], jnp.log1p(-lb)[:, None], (1.0 - lb)[:, None], pad], axis=1)


def kernel(x, c, ctx, c_ctx, w_mod, b_mod, norm_pre_mix, norm_post_mix, norm_pre_ffn, norm_post_ffn, even_w_in, even_w_out, even_q_norm, even_k_norm, hgrn_lb_logits, hgrn_out_norm, odd_w_qkv, odd_w_out, odd_rpb, ffn_w_up, ffn_conv_w, ffn_conv_b, ffn_w_down):
    bsz, seq, d = x.shape
    n_ctx = ctx.shape[1]
    depth = w_mod.shape[0]
    n_even = even_w_in.shape[0]

    modtab = _modulation(c, c_ctx, w_mod, b_mod)
    ctx_row = bsz
    rope = _rope_tables(seq)
    gmat = jnp.kron(jnp.eye(LANES // HEAD_DIM, dtype=F32), jnp.full((HEAD_DIM, HEAD_DIM), 1.0 / HEAD_DIM, F32))
    tri, sel = _hgrn_tables()
    p_lb = jax.nn.softmax(hgrn_lb_logits.astype(F32), axis=0)
    lb_all = jnp.concatenate([jnp.zeros_like(p_lb[:1]), jnp.cumsum(p_lb[1:], axis=0)], axis=0)

    xl = x.reshape(bsz * seq, d)
    xc = ctx.reshape(bsz * n_ctx, d)
    tm_lat = min(512, seq)
    tm_ctx = min(512, bsz * n_ctx)
    lat_kw = dict(seq_len=seq, mod_row=None, tm=tm_lat)
    ctx_kw = dict(seq_len=n_ctx, mod_row=ctx_row, tm=tm_ctx)

    for l in range(depth):
        with_ctx_out = l < depth - 1
        mt = modtab[l]
        if l % 2 == 0:
            e = l // 2
            w_in = even_w_in[e].astype(BF16)
            w_out = even_w_out[e].astype(BF16)
            qn = jnp.tile(even_q_norm[e], LANES // HEAD_DIM).reshape(1, LANES)
            kn = jnp.tile(even_k_norm[e], LANES // HEAD_DIM).reshape(1, LANES)
            pl_ = _even_in(xl, mt, norm_pre_mix[l], w_in, qn, kn, rope, gmat, **lat_kw)
            pc_ = _even_in(xc, mt, norm_pre_mix[l], w_in, qn, kn, None, gmat, **ctx_kw)
            qa_l, ka_l, va_l, bq_l, z_l, bi_l, sg_l = pl_
            qa_c, ka_c, va_c, bq_c, z_c, bi_c, sg_c = pc_
            r3 = lambda a, n: a.reshape(bsz, n, a.shape[-1])
            oa_l = _gqa(r3(qa_l, seq), r3(ka_c, n_ctx), r3(va_c, n_ctx), r3(ka_l, seq), r3(va_l, seq),
                        tq=min(256, seq))
            lbtab = _lb_table(lb_all[e])
            s0 = jnp.zeros((bsz, 2, B_HEADS, LANES, LANES), F32)
            ob_c, s_ctx = _hgrn(r3(bq_c, n_ctx), r3(z_c, n_ctx), r3(bi_c, n_ctx), lbtab, tri, sel, s0)
            ob_l, _ = _hgrn(r3(bq_l, seq), r3(z_l, seq), r3(bi_l, seq), lbtab, tri, sel, s_ctx)
            xl_new = _even_out(xl, mt, oa_l.reshape(bsz * seq, A_Q_W), ob_l.reshape(2, bsz * seq, B_W), sg_l,
                               hgrn_out_norm[e], w_out, norm_post_mix[l], **lat_kw)
            if with_ctx_out:
                oa_c = _gqa(r3(qa_c, n_ctx), r3(ka_c, n_ctx), r3(va_c, n_ctx), tq=n_ctx)
                xc = _even_out(xc, mt, oa_c.reshape(bsz * n_ctx, A_Q_W), ob_c.reshape(2, bsz * n_ctx, B_W), sg_c,
                               hgrn_out_norm[e], w_out, norm_post_mix[l], **ctx_kw)
            xl = xl_new
        else:
            o = l // 2
            w_qkv = odd_w_qkv[o].astype(BF16)
            w_out = odd_w_out[o].astype(BF16)
            ql, kl, vl = _odd_in(xl, mt, norm_pre_mix[l], w_qkv, **lat_kw)
            qc, kc, vc = _odd_in(xc, mt, norm_pre_mix[l], w_qkv, **ctx_kw)
            r3 = lambda a, n: a.reshape(bsz, n, a.shape[-1])
            o_l, o_c = _na(r3(ql, seq), r3(kl, seq), r3(vl, seq), r3(qc, n_ctx), r3(kc, n_ctx), r3(vc, n_ctx),
                           _na_bias(odd_rpb[o]))
            xl = _odd_out(xl, mt, o_l.reshape(bsz * seq, -1), w_out, norm_post_mix[l], **lat_kw)
            if with_ctx_out:
                xc = _odd_out(xc, mt, o_c.reshape(bsz * n_ctx, -1), w_out, norm_post_mix[l], **ctx_kw)

        w_up = ffn_w_up[l].astype(BF16)
        w_dn = ffn_w_down[l].astype(BF16)
        ffn_args = (norm_pre_ffn[l], w_up, ffn_conv_w[l], ffn_conv_b[l], w_dn, norm_post_ffn[l])
        xl = _ffn(xl, mt, *ffn_args, **lat_kw)
        if with_ctx_out:
            xc = _ffn(xc, mt, *ffn_args, **ctx_kw)
    return xl.reshape(bsz, seq, d)
```

```python
import functools

import numpy as np
import jax
import jax.numpy as jnp
from jax import lax
from jax.experimental import pallas as pl
from jax.experimental.pallas import tpu as pltpu

F32 = jnp.float32
BF16 = jnp.bfloat16
HIGHEST = lax.Precision.HIGHEST

V7X_VMEM_BYTES = 64 * 1024 * 1024
VMEM_LIMIT_BYTES = V7X_VMEM_BYTES - 8 * 1024 * 1024
LANES = 128
BF16_SUBLANES = 16

GRID_W = 64
HEAD_DIM = 64
A_Q_HEADS = 8
A_KV_HEADS = 2
A_GROUP = A_Q_HEADS // A_KV_HEADS
B_HEADS = 4
B_KEY_DIM = 128
C_HEADS = 16
NA_ROWS = 8
NA_COLS = 16
EPS = 1e-6
ROPE_THETA = 10000.0
ATTN_SCALE = HEAD_DIM ** -0.5
HGRN_SCALE = B_KEY_DIM ** -0.5
NEG_BIG = -1e30

A_Q_W = A_Q_HEADS * HEAD_DIM
A_KV_W = A_KV_HEADS * HEAD_DIM
B_W = B_HEADS * B_KEY_DIM

HGRN_CHUNK = 64
HGRN_SUB = 16
HGRN_BLOCK = 256
KV_BLOCK = 256
GQA_TQ = 128
GQA_UNROLL = 4
NA_ROW_UNROLL = 4


def _cparams(*sem):
    return pltpu.CompilerParams(dimension_semantics=sem, vmem_limit_bytes=VMEM_LIMIT_BYTES)


def _const_spec(shape):
    n = len(shape)
    return pl.BlockSpec(shape, lambda *_: (0,) * n)


def _silu(x):
    return x / (1.0 + jnp.exp(-x))


def _rms(x, w):
    return x * lax.rsqrt(jnp.mean(x * x, axis=-1, keepdims=True) + EPS) * w


def _norm_mod(x, w, scale, shift):
    return _rms(x, w) * (1.0 + scale) + shift


def _mod_body(c_ref, w_ref, b_ref, o_ref):
    s = _silu(c_ref[...])
    o_ref[0] = jnp.dot(s, w_ref[0], preferred_element_type=F32, precision=HIGHEST) + b_ref[0]


def _modulation(c, c_ctx, w_mod, b_mod):
    depth, d, n = w_mod.shape
    b = c.shape[0]
    rows = -(-(b + 1) // 8) * 8
    cc = jnp.concatenate([c, c_ctx[None, :], jnp.zeros((rows - b - 1, d), F32)], axis=0)
    tn = n // 4
    out = pl.pallas_call(
        _mod_body,
        grid=(depth, n // tn),
        in_specs=[
            pl.BlockSpec((rows, d), lambda l, j: (0, 0)),
            pl.BlockSpec((1, d, tn), lambda l, j: (l, 0, j)),
            pl.BlockSpec((1, 1, tn), lambda l, j: (l, 0, j)),
        ],
        out_specs=pl.BlockSpec((1, rows, tn), lambda l, j: (l, 0, j)),
        out_shape=jax.ShapeDtypeStruct((depth, rows, n), F32),
        compiler_params=_cparams("parallel", "parallel"),
        name="adaln_mod",
    )(cc, w_mod, b_mod.reshape(depth, 1, n))
    return out.reshape(depth, rows, 6, d)


def _mod_index(seq_len, tm, mod_row):
    if mod_row is None:
        per_seq = seq_len // tm
        return lambda i: (i // per_seq, 0, 0)
    return lambda i: (mod_row, 0, 0)


def _even_in_body(x_ref, mod_ref, nw_ref, w_ref, qw_ref, kw_ref, cos_ref, sin_ref, gmat_ref,
                  qa_ref, ka_ref, va_ref, bq_ref, z_ref, bi_ref, sg_ref, *, use_rope):
    h = _norm_mod(x_ref[...], nw_ref[...], mod_ref[0, 1:2, :], mod_ref[0, 0:1, :]).astype(BF16)
    tm = h.shape[0]
    lane = lax.broadcasted_iota(jnp.int32, (tm, LANES), 1)
    lo_half = (lane % HEAD_DIM) < (HEAD_DIM // 2)
    lo_head = lane < HEAD_DIM
    gmat = gmat_ref[...]

    def proj(a, b):
        return jnp.dot(h, w_ref[:, a:b], preferred_element_type=F32)

    def head_norm_rope(zc, w):
        ms = jnp.dot(zc * zc, gmat, preferred_element_type=F32, precision=HIGHEST)
        y = zc * lax.rsqrt(ms + EPS) * w
        if use_rope:
            rot = jnp.where(lo_half, pltpu.roll(y, LANES - HEAD_DIM // 2, 1), pltpu.roll(y, HEAD_DIM // 2, 1))
            y = y * cos_ref[...] + rot * sin_ref[...]
        return y

    zq = proj(0, A_Q_W)
    for c in range(A_Q_W // LANES):
        y = head_norm_rope(zq[:, c * LANES:(c + 1) * LANES], qw_ref[...]) * ATTN_SCALE
        swapped = pltpu.roll(y, HEAD_DIM, 1)
        g = (2 * c) // A_GROUP
        if g == 0:
            h0, h1 = jnp.where(lo_head, y, 0.0), jnp.where(lo_head, swapped, 0.0)
        else:
            h0, h1 = jnp.where(lo_head, 0.0, swapped), jnp.where(lo_head, 0.0, y)
        qa_ref[:, (2 * c) * LANES:(2 * c + 1) * LANES] = h0.astype(BF16)
        qa_ref[:, (2 * c + 1) * LANES:(2 * c + 2) * LANES] = h1.astype(BF16)
    o = A_Q_W
    ka_ref[...] = head_norm_rope(proj(o, o + A_KV_W), kw_ref[...]).astype(BF16)
    o += A_KV_W
    va_ref[...] = proj(o, o + A_KV_W).astype(BF16)
    o += A_KV_W
    bq_ref[...] = (_silu(proj(o, o + B_W)) * HGRN_SCALE).astype(BF16)
    o += B_W
    z_ref[...] = proj(o, o + 2 * B_W)
    o += 2 * B_W
    bi_ref[...] = proj(o, o + B_W).astype(BF16)
    o += B_W
    sg_ref[...] = _silu(proj(o, o + B_W)).astype(BF16)


def _even_in(x, modtab, norm_w, w_in, q_norm, k_norm, rope, gmat, *, seq_len, mod_row, tm):
    n, d = x.shape
    use_rope = rope is not None
    cos, sin = rope if use_rope else (jnp.zeros((8, LANES), F32),) * 2
    per_seq = seq_len // tm
    rope_spec = pl.BlockSpec((tm, LANES), lambda i: (i % per_seq, 0)) if use_rope else _const_spec((8, LANES))
    row = lambda w: pl.BlockSpec((tm, w), lambda i: (i, 0))
    widths = (2 * A_Q_W, A_KV_W, A_KV_W, B_W, 2 * B_W, B_W, B_W)
    dtypes = (BF16, BF16, BF16, BF16, F32, BF16, BF16)
    return pl.pallas_call(
        functools.partial(_even_in_body, use_rope=use_rope),
        grid=(n // tm,),
        in_specs=[
            row(d),
            pl.BlockSpec((1, 6, d), _mod_index(seq_len, tm, mod_row)),
            _const_spec((1, d)),
            _const_spec(w_in.shape),
            _const_spec((1, LANES)),
            _const_spec((1, LANES)),
            rope_spec, rope_spec,
            _const_spec((LANES, LANES)),
        ],
        out_specs=[row(w) for w in widths],
        out_shape=[jax.ShapeDtypeStruct((n, w), t) for w, t in zip(widths, dtypes)],
        compiler_params=_cparams("parallel"),
        name="even_in",
    )(x, modtab, norm_w.reshape(1, d), w_in, q_norm, k_norm, cos, sin, gmat)


def _gqa_body(*refs, tq, n_lat_blk):
    if n_lat_blk:
        q_ref, kc_ref, vc_ref, kl_ref, vl_ref, o_ref, s_ref, m_ref, acc_ref = refs
    else:
        q_ref, kc_ref, vc_ref, o_ref, s_ref, m_ref, acc_ref = refs
    lo_head = lax.broadcasted_iota(jnp.int32, (tq, LANES), 1) < HEAD_DIM
    lo_head_kv = lax.broadcasted_iota(jnp.int32, (KV_BLOCK, LANES), 1) < HEAD_DIM
    for g in range(A_KV_HEADS):
        q = jnp.concatenate(
            [q_ref[0, :, (A_GROUP * g + j) * LANES:(A_GROUP * g + j + 1) * LANES] for j in range(A_GROUP)], axis=0)
        own = lo_head_kv if g == 0 else ~lo_head_kv

        def scores(j, k):
            s = lax.dot_general(q, k, (((1,), (1,)), ((), ())), preferred_element_type=F32)
            s_ref[j] = s
            blk_max = s[:, 0:LANES]
            for c in range(1, KV_BLOCK // LANES):
                blk_max = jnp.maximum(blk_max, s[:, c * LANES:(c + 1) * LANES])
            return blk_max

        def lat_rows(j):
            return pl.ds(pl.multiple_of(j * KV_BLOCK, KV_BLOCK), KV_BLOCK)

        m_ref[...] = scores(0, kc_ref[0])
        if n_lat_blk:
            def score_step(i, carry):
                blk_max = None
                for u in range(GQA_UNROLL):
                    j = i * GQA_UNROLL + u
                    cur = scores(j + 1, kl_ref[0, lat_rows(j), :])
                    blk_max = cur if blk_max is None else jnp.maximum(blk_max, cur)
                m_ref[...] = jnp.maximum(m_ref[...], blk_max)
                return carry
            lax.fori_loop(0, n_lat_blk // GQA_UNROLL, score_step, 0)
        m_ref[...] = jnp.broadcast_to(jnp.max(m_ref[...], axis=1, keepdims=True), m_ref.shape)

        def weighted(j, v):
            m = m_ref[...]
            p = jnp.exp(s_ref[j] - jnp.concatenate([m] * (KV_BLOCK // LANES), axis=1))
            return jnp.dot(p.astype(BF16), jnp.where(own, v, jnp.ones_like(v)), preferred_element_type=F32)

        acc_ref[...] = weighted(0, vc_ref[0])
        if n_lat_blk:
            def pv_step(i, carry):
                part = None
                for u in range(GQA_UNROLL):
                    j = i * GQA_UNROLL + u
                    cur = weighted(j + 1, vl_ref[0, lat_rows(j), :])
                    part = cur if part is None else part + cur
                acc_ref[...] += part
                return carry
            lax.fori_loop(0, n_lat_blk // GQA_UNROLL, pv_step, 0)

        acc = acc_ref[...]
        o = acc / pltpu.roll(acc, HEAD_DIM, 1)
        for c in range(A_GROUP // 2):
            o0 = o[(2 * c) * tq:(2 * c + 1) * tq]
            o1 = o[(2 * c + 1) * tq:(2 * c + 2) * tq]
            if g == 0:
                pair = jnp.where(lo_head, o0, pltpu.roll(o1, HEAD_DIM, 1))
            else:
                pair = jnp.where(lo_head, pltpu.roll(o0, HEAD_DIM, 1), o1)
            col = (A_GROUP // 2 * g + c) * LANES
            o_ref[0, :, col:col + LANES] = pair.astype(BF16)


def _gqa(q, k_ctx, v_ctx, k_lat=None, v_lat=None, *, tq):
    b, nq, _ = q.shape
    n_ctx = k_ctx.shape[1]
    assert n_ctx == KV_BLOCK
    with_lat = k_lat is not None
    n_lat = k_lat.shape[1] if with_lat else 0
    kv_spec = lambda n: pl.BlockSpec((1, n, A_KV_W), lambda i, j: (i, 0, 0))
    in_specs = [pl.BlockSpec((1, tq, 2 * A_Q_W), lambda i, j: (i, j, 0)), kv_spec(n_ctx), kv_spec(n_ctx)]
    args = [q, k_ctx, v_ctx]
    if with_lat:
        in_specs += [kv_spec(n_lat), kv_spec(n_lat)]
        args += [k_lat, v_lat]
    rows = A_GROUP * tq
    n_blk = 1 + n_lat // KV_BLOCK
    assert (n_blk - 1) % GQA_UNROLL == 0
    return pl.pallas_call(
        functools.partial(_gqa_body, tq=tq, n_lat_blk=n_lat // KV_BLOCK),
        grid=(b, nq // tq),
        in_specs=in_specs,
        out_specs=pl.BlockSpec((1, tq, A_Q_W), lambda i, j: (i, j, 0)),
        out_shape=jax.ShapeDtypeStruct((b, nq, A_Q_W), BF16),
        scratch_shapes=[pltpu.VMEM((n_blk, rows, KV_BLOCK), F32), pltpu.VMEM((rows, LANES), F32),
                        pltpu.VMEM((rows, LANES), F32)],
        compiler_params=_cparams("parallel", "parallel"),
        name="gqa_lat" if with_lat else "gqa_ctx",
    )(*args)


def _hgrn_masks(rev):
    c = HGRN_CHUNK
    t = lax.broadcasted_iota(jnp.int32, (c, c), 0)
    s = lax.broadcasted_iota(jnp.int32, (c, c), 1)
    causal = (s >= t) if rev else (s <= t)
    diag = ((t // HGRN_SUB) == (s // HGRN_SUB)) & causal
    levels = []
    size = 2 * HGRN_SUB
    while size <= c:
        half = size // 2
        t_hi, s_hi = (t % size) >= half, (s % size) >= half
        pair = (~t_hi & s_hi) if rev else (t_hi & ~s_hi)
        levels.append((size, ((t // size) == (s // size)) & pair))
        size *= 2
    return diag, levels


def _hgrn_chunk(row0, rev, q_ref, z_ref, v_ref, lb_ref, tri_ref, sel_ref, o_ref, st_ref, kbuf, bbuf, pcat):
    c, sub = HGRN_CHUNK, HGRN_SUB
    rows = pl.ds(row0, c)
    diag_mask, level_masks = _hgrn_masks(rev)
    tri = tri_ref[...]
    per_head = []
    for hd in range(B_HEADS):
        cols = slice(hd * LANES, (hd + 1) * LANES)
        q = q_ref[0, rows, cols].astype(F32)
        z = z_ref[0, rows, cols]
        log_lb, log_1m_lb, one_m_lb = lb_ref[0, 0:1, cols], lb_ref[0, 1:2, cols], lb_ref[0, 2:3, cols]
        kg = one_m_lb / (1.0 + jnp.exp(z))
        cterm = log_1m_lb + (jnp.minimum(z, 0.0) - jnp.log1p(jnp.exp(-jnp.abs(z))))
        log_f = jnp.maximum(log_lb, cterm) + jnp.log1p(jnp.exp(-jnp.abs(log_lb - cterm)))
        b = jnp.dot(tri, log_f, preferred_element_type=F32, precision=HIGHEST)
        kbuf[hd] = kg
        bbuf[hd] = b
        for u in range(sub):
            ku = jnp.concatenate(
                [jnp.broadcast_to(kbuf[hd, m * sub + u:m * sub + u + 1, :], (sub, LANES)) for m in range(c // sub)], 0)
            bu = jnp.concatenate(
                [jnp.broadcast_to(bbuf[hd, m * sub + u:m * sub + u + 1, :], (sub, LANES)) for m in range(c // sub)], 0)
            pcat[hd * c:(hd + 1) * c, u * LANES:(u + 1) * LANES] = (
                q * ku * jnp.exp(jnp.minimum(b - bu, 0.0))).astype(BF16)
        per_head.append((q, kg, b))
    a_diag = jnp.dot(pcat[...], sel_ref[...], preferred_element_type=F32)
    for hd in range(B_HEADS):
        cols = slice(hd * LANES, (hd + 1) * LANES)
        q, kg, b = per_head[hd]
        v = v_ref[0, rows, cols]
        a = jnp.where(diag_mask, a_diag[hd * c:(hd + 1) * c], 0.0)
        for size, mask in level_masks:
            half = size // 2
            ref_rows = [m * size + (half if rev else half - 1) for m in range(c // size)]
            beta = jnp.concatenate(
                [jnp.broadcast_to(bbuf[hd, r:r + 1, :], (size, LANES)) for r in ref_rows], 0)
            e = jnp.exp(-jnp.abs(b - beta))
            lvl = lax.dot_general((q * e).astype(BF16), (kg * e).astype(BF16), (((1,), (1,)), ((), ())),
                                  preferred_element_type=F32)
            a = a + jnp.where(mask, lvl, 0.0)
        b_end = bbuf[hd, 0:1, :] if rev else bbuf[hd, c - 1:c, :]
        st = st_ref[0, 0, hd]
        o = jnp.dot(a.astype(BF16), v, preferred_element_type=F32)
        o = o + lax.dot_general((q * jnp.exp(b)).astype(BF16), st.astype(BF16), (((1,), (1,)), ((), ())),
                                preferred_element_type=F32)
        o_ref[0, 0, rows, cols] = o
        kd = (kg * jnp.exp(b_end - b)).astype(BF16)
        upd = lax.dot_general(v, kd, (((0,), (0,)), ((), ())), preferred_element_type=F32)
        st_ref[0, 0, hd] = st * jnp.exp(b_end) + upd


def _hgrn_body(q_ref, z_ref, v_ref, lb_ref, tri_ref, sel_ref, s0_ref, o_ref, st_ref, kbuf, bbuf, pcat):
    d, j = pl.program_id(1), pl.program_id(2)
    n_chunks = HGRN_BLOCK // HGRN_CHUNK

    @pl.when(j == 0)
    def _():
        st_ref[...] = s0_ref[...]

    for rev in (False, True):
        @pl.when(d == int(rev))
        def _(rev=rev):
            def chunk(i, carry):
                ci = (n_chunks - 1 - i) if rev else i
                row0 = pl.multiple_of(ci * HGRN_CHUNK, HGRN_CHUNK)
                _hgrn_chunk(row0, rev, q_ref, z_ref, v_ref, lb_ref, tri_ref, sel_ref, o_ref, st_ref,
                            kbuf, bbuf, pcat)
                return carry
            lax.fori_loop(0, n_chunks, chunk, 0)


def _hgrn(q, z, v, lbtab, tri, sel, s0):
    b, n, _ = q.shape
    nblk = n // HGRN_BLOCK
    c = HGRN_CHUNK

    def blk(d, j):
        return jnp.where(d == 0, j, nblk - 1 - j)

    tok = lambda i, d, j: (i, blk(d, j), 0)
    st_spec = pl.BlockSpec((1, 1, B_HEADS, LANES, LANES), lambda i, d, j: (i, d, 0, 0, 0))
    return pl.pallas_call(
        _hgrn_body,
        grid=(b, 2, nblk),
        in_specs=[
            pl.BlockSpec((1, HGRN_BLOCK, B_W), tok),
            pl.BlockSpec((1, HGRN_BLOCK, B_W), lambda i, d, j: (i, blk(d, j), d)),
            pl.BlockSpec((1, HGRN_BLOCK, B_W), tok),
            pl.BlockSpec((1, 8, B_W), lambda i, d, j: (d, 0, 0)),
            pl.BlockSpec((None, c, c), lambda i, d, j: (d, 0, 0)),
            _const_spec(sel.shape),
            st_spec,
        ],
        out_specs=[
            pl.BlockSpec((1, 1, HGRN_BLOCK, B_W), lambda i, d, j: (d, i, blk(d, j), 0)),
            st_spec,
        ],
        out_shape=[jax.ShapeDtypeStruct((2, b, n, B_W), F32),
                   jax.ShapeDtypeStruct(s0.shape, F32)],
        scratch_shapes=[pltpu.VMEM((B_HEADS, c, LANES), F32), pltpu.VMEM((B_HEADS, c, LANES), F32),
                        pltpu.VMEM((B_HEADS * c, HGRN_SUB * LANES), BF16)],
        compiler_params=_cparams("parallel", "parallel", "arbitrary"),
        name="hgrn_scan",
    )(q, z, v, lbtab, tri, sel, s0)


def _odd_in_body(x_ref, mod_ref, nw_ref, w_ref, q_ref, k_ref, v_ref):
    h = _norm_mod(x_ref[...], nw_ref[...], mod_ref[0, 1:2, :], mod_ref[0, 0:1, :]).astype(BF16)
    cw = q_ref.shape[1]
    q_ref[...] = (jnp.dot(h, w_ref[:, 0:cw], preferred_element_type=F32) * ATTN_SCALE).astype(BF16)
    k_ref[...] = jnp.dot(h, w_ref[:, cw:2 * cw], preferred_element_type=F32).astype(BF16)
    v_ref[...] = jnp.dot(h, w_ref[:, 2 * cw:3 * cw], preferred_element_type=F32).astype(BF16)


def _odd_in(x, modtab, norm_w, w_qkv, *, seq_len, mod_row, tm):
    n, d = x.shape
    cw = w_qkv.shape[1] // 3
    row = pl.BlockSpec((tm, cw), lambda i: (i, 0))
    return pl.pallas_call(
        _odd_in_body,
        grid=(n // tm,),
        in_specs=[
            pl.BlockSpec((tm, d), lambda i: (i, 0)),
            pl.BlockSpec((1, 6, d), _mod_index(seq_len, tm, mod_row)),
            _const_spec((1, d)),
            _const_spec(w_qkv.shape),
        ],
        out_specs=[row, row, row],
        out_shape=[jax.ShapeDtypeStruct((n, cw), BF16)] * 3,
        compiler_params=_cparams("parallel"),
        name="odd_in",
    )(x, modtab, norm_w.reshape(1, d), w_qkv)


def _na_body(ql_ref, kl_ref, vl_ref, qc_ref, kc_ref, vc_ref, bias_ref, ol_ref, oc_ref, *, rows):
    band = NA_ROWS * GRID_W
    lane_q = lax.broadcasted_iota(jnp.int32, (GRID_W, LANES), 1) < HEAD_DIM
    kc, vc = kc_ref[0], vc_ref[0]
    nt = (((1,), (1,)), ((), ()))

    def two_heads(q, lo_mask):
        zero = jnp.zeros_like(q)
        return jnp.concatenate([jnp.where(lo_mask, q, zero), jnp.where(lo_mask, zero, q)], axis=0)

    def merge(o, n, lo_mask):
        return jnp.where(lo_mask, o[:n], o[n:])

    def with_ones(v):
        return jnp.concatenate([v, jnp.ones_like(v)], axis=1)

    vc_aug = with_ones(vc)

    def row_group(i, carry):
        n2 = 2 * GRID_W
        rs = [i * NA_ROW_UNROLL + u for u in range(NA_ROW_UNROLL)]
        q = jnp.concatenate(
            [two_heads(ql_ref[0, pl.ds(pl.multiple_of(r * GRID_W, GRID_W), GRID_W), :], lane_q) for r in rs], axis=0)
        s_ctx = lax.dot_general(q, kc, nt, preferred_element_type=F32)
        p_ctx, p_loc, v_band = [], [], []
        for u, r in enumerate(rs):
            r0 = jnp.clip(r - NA_ROWS // 2, 0, rows - NA_ROWS)
            kstart = pl.multiple_of(r0 * GRID_W, GRID_W)
            s_loc = lax.dot_general(q[u * n2:(u + 1) * n2], kl_ref[0, pl.ds(kstart, band), :], nt,
                                    preferred_element_type=F32) + bias_ref[0, r - r0]
            s_c = s_ctx[u * n2:(u + 1) * n2]
            m = jnp.maximum(jnp.max(s_loc, axis=1, keepdims=True), jnp.max(s_c, axis=1, keepdims=True))
            p_loc.append(jnp.exp(s_loc - m).astype(BF16))
            p_ctx.append(jnp.exp(s_c - m).astype(BF16))
            v_band.append(vl_ref[0, pl.ds(kstart, band), :])
        o_ctx = jnp.dot(jnp.concatenate(p_ctx, axis=0), vc_aug, preferred_element_type=F32)
        for u, r in enumerate(rs):
            o = jnp.dot(p_loc[u], with_ones(v_band[u]), preferred_element_type=F32) + o_ctx[u * n2:(u + 1) * n2]
            o = o[:, :LANES] / o[:, LANES:]
            ol_ref[0, pl.ds(pl.multiple_of(r * GRID_W, GRID_W), GRID_W), :] = (
                merge(o, GRID_W, lane_q).astype(BF16))
        return carry

    lax.fori_loop(0, rows // NA_ROW_UNROLL, row_group, 0)

    n_ctx = qc_ref.shape[1]
    lane_c = lax.broadcasted_iota(jnp.int32, (n_ctx, LANES), 1) < HEAD_DIM
    q = two_heads(qc_ref[0], lane_c)
    s = lax.dot_general(q, kc, nt, preferred_element_type=F32)
    p = jnp.exp(s - jnp.max(s, axis=1, keepdims=True)).astype(BF16)
    o = jnp.dot(p, vc_aug, preferred_element_type=F32)
    oc_ref[0] = merge(o[:, :LANES] / o[:, LANES:], n_ctx, lane_c).astype(BF16)


def _na_bias(rpb):
    heads, n_dy, n_dx = rpb.shape
    qc = np.arange(GRID_W)
    kc = np.arange(GRID_W)
    c_start = np.clip(qc - NA_COLS // 2, 0, GRID_W - NA_COLS)
    in_win = (kc[None, :] >= c_start[:, None]) & (kc[None, :] < c_start[:, None] + NA_COLS)
    dx = kc[None, :] - qc[:, None] + (NA_COLS - 1)
    onehot = ((dx[:, :, None] == np.arange(n_dx)[None, None, :]) & in_win[:, :, None]).astype(np.float32)
    t = jnp.einsum("hyx,qcx->hqyc", rpb.astype(F32), jnp.asarray(onehot), precision=HIGHEST)
    t = jnp.where(jnp.asarray(in_win)[None, :, None, :], t, NEG_BIG).reshape(heads, GRID_W, n_dy * GRID_W)
    width = NA_ROWS * GRID_W
    tab = jnp.stack([t[:, :, (NA_ROWS - 1 - var) * GRID_W:(NA_ROWS - 1 - var) * GRID_W + width]
                     for var in range(NA_ROWS)], axis=1)
    tab = tab.reshape(heads // 2, 2, NA_ROWS, GRID_W, width)
    return tab.transpose(0, 2, 1, 3, 4).reshape(heads // 2, NA_ROWS, 2 * GRID_W, width)


def _na(q_lat, k_lat, v_lat, q_ctx, k_ctx, v_ctx, bias):
    b, n, w = q_lat.shape
    n_ctx = q_ctx.shape[1]
    rows = n // GRID_W
    assert rows >= NA_ROWS and rows % NA_ROW_UNROLL == 0
    pairs = w // LANES
    lat = pl.BlockSpec((1, n, LANES), lambda p, i: (i, 0, p))
    ctx = pl.BlockSpec((1, n_ctx, LANES), lambda p, i: (i, 0, p))
    return pl.pallas_call(
        functools.partial(_na_body, rows=rows),
        grid=(pairs, b),
        in_specs=[lat, lat, lat, ctx, ctx, ctx,
                  pl.BlockSpec((1,) + bias.shape[1:], lambda p, i: (p, 0, 0, 0))],
        out_specs=[lat, ctx],
        out_shape=[jax.ShapeDtypeStruct((b, n, w), BF16), jax.ShapeDtypeStruct((b, n_ctx, w), BF16)],
        compiler_params=_cparams("parallel", "parallel"),
        name="na_attn",
    )(q_lat, k_lat, v_lat, q_ctx, k_ctx, v_ctx, bias)


def _post(acc, post_w, gate, x):
    return x + gate * _rms(acc, post_w)


def _even_out_body(x_ref, mod_ref, oa_ref, ob_ref, sg_ref, onw_ref, w_ref, pw_ref, y_ref):
    acc = jnp.dot(oa_ref[...], w_ref[0:A_Q_W, :], preferred_element_type=F32)
    ob = ob_ref[0] + ob_ref[1]
    parts = []
    for hd in range(B_HEADS):
        cols = slice(hd * LANES, (hd + 1) * LANES)
        parts.append((_rms(ob[:, cols], onw_ref[...]) * sg_ref[:, cols].astype(F32)).astype(BF16))
    acc = acc + jnp.dot(jnp.concatenate(parts, axis=1), w_ref[A_Q_W:, :], preferred_element_type=F32)
    y_ref[...] = _post(acc, pw_ref[...], mod_ref[0, 2:3, :], x_ref[...])


def _even_out(x, modtab, oa, ob, sg, out_norm, w_out, post_w, *, seq_len, mod_row, tm):
    n, d = x.shape
    row = lambda w: pl.BlockSpec((tm, w), lambda i: (i, 0))
    return pl.pallas_call(
        _even_out_body,
        grid=(n // tm,),
        in_specs=[
            row(d),
            pl.BlockSpec((1, 6, d), _mod_index(seq_len, tm, mod_row)),
            row(A_Q_W),
            pl.BlockSpec((2, tm, B_W), lambda i: (0, i, 0)),
            row(B_W),
            _const_spec((1, LANES)),
            _const_spec(w_out.shape),
            _const_spec((1, d)),
        ],
        out_specs=row(d),
        out_shape=jax.ShapeDtypeStruct((n, d), F32),
        compiler_params=_cparams("parallel"),
        name="even_out",
    )(x, modtab, oa, ob, sg, out_norm.reshape(1, LANES), w_out, post_w.reshape(1, d))


def _odd_out_body(x_ref, mod_ref, o_ref, w_ref, pw_ref, y_ref):
    acc = jnp.dot(o_ref[...], w_ref[...], preferred_element_type=F32)
    y_ref[...] = _post(acc, pw_ref[...], mod_ref[0, 2:3, :], x_ref[...])


def _odd_out(x, modtab, o, w_out, post_w, *, seq_len, mod_row, tm):
    n, d = x.shape
    row = lambda w: pl.BlockSpec((tm, w), lambda i: (i, 0))
    return pl.pallas_call(
        _odd_out_body,
        grid=(n // tm,),
        in_specs=[
            row(d),
            pl.BlockSpec((1, 6, d), _mod_index(seq_len, tm, mod_row)),
            row(o.shape[1]),
            _const_spec(w_out.shape),
            _const_spec((1, d)),
        ],
        out_specs=row(d),
        out_shape=jax.ShapeDtypeStruct((n, d), F32),
        compiler_params=_cparams("parallel"),
        name="odd_out",
    )(x, modtab, o, w_out, post_w.reshape(1, d))


FFN_HALO = BF16_SUBLANES
FFN_TN = 256


def _ffn_body(x_ref, xp_ref, xn_ref, mod_ref, nw_ref, wup_ref, cw_ref, cb_ref, wdn_ref, pw_ref, y_ref,
              hbuf, ua, ug, acc, *, seq_len):
    tm, d = x_ref.shape
    dff = wdn_ref.shape[0]
    halo = FFN_HALO
    scale, shift = mod_ref[0, 4:5, :], mod_ref[0, 3:4, :]
    x = x_ref[...]
    hbuf[0:halo, :] = _norm_mod(xp_ref[...], nw_ref[...], scale, shift).astype(BF16)
    hbuf[halo:halo + tm, :] = _norm_mod(x, nw_ref[...], scale, shift).astype(BF16)
    hbuf[halo + tm:, :] = _norm_mod(xn_ref[...], nw_ref[...], scale, shift).astype(BF16)
    pos = (pl.program_id(0) * tm + lax.broadcasted_iota(jnp.int32, (tm, FFN_TN), 0)) % seq_len
    has_prev, has_next = pos != 0, pos != seq_len - 1
    acc[...] = jnp.zeros(acc.shape, F32)
    h = hbuf[...]
    for j in range(dff // FFN_TN):
        halves = []
        for buf, base in ((ua, 0), (ug, dff)):
            cols = slice(base + j * FFN_TN, base + (j + 1) * FFN_TN)
            buf[...] = jnp.dot(h, wup_ref[:, cols], preferred_element_type=F32)
            halves.append(
                cw_ref[0:1, cols] * jnp.where(has_prev, buf[halo - 1:halo - 1 + tm, :], 0.0)
                + cw_ref[1:2, cols] * buf[halo:halo + tm, :]
                + cw_ref[2:3, cols] * jnp.where(has_next, buf[halo + 1:halo + 1 + tm, :], 0.0)
                + cb_ref[0:1, cols])
        a, g = halves
        act = (_silu(g) * a).astype(BF16)
        acc[...] += jnp.dot(act, wdn_ref[j * FFN_TN:(j + 1) * FFN_TN, :], preferred_element_type=F32)
    y_ref[...] = _post(acc[...], pw_ref[...], mod_ref[0, 5:6, :], x)


def _ffn(x, modtab, norm_w, w_up, conv_w, conv_b, w_down, post_w, *, seq_len, mod_row, tm):
    n, d = x.shape
    dff = w_down.shape[0]
    assert dff % FFN_TN == 0 and tm % FFN_HALO == 0
    per_tile = tm // FFN_HALO
    last = n // FFN_HALO - 1
    return pl.pallas_call(
        functools.partial(_ffn_body, seq_len=seq_len),
        grid=(n // tm,),
        in_specs=[
            pl.BlockSpec((tm, d), lambda i: (i, 0)),
            pl.BlockSpec((FFN_HALO, d), lambda i: (jnp.maximum(i * per_tile - 1, 0), 0)),
            pl.BlockSpec((FFN_HALO, d), lambda i: (jnp.minimum((i + 1) * per_tile, last), 0)),
            pl.BlockSpec((1, 6, d), _mod_index(seq_len, tm, mod_row)),
            _const_spec((1, d)),
            _const_spec(w_up.shape),
            _const_spec(conv_w.shape),
            _const_spec((1, 2 * dff)),
            _const_spec(w_down.shape),
            _const_spec((1, d)),
        ],
        out_specs=pl.BlockSpec((tm, d), lambda i: (i, 0)),
        out_shape=jax.ShapeDtypeStruct((n, d), F32),
        scratch_shapes=[pltpu.VMEM((tm + 2 * FFN_HALO, d), BF16),
                        pltpu.VMEM((tm + 2 * FFN_HALO, FFN_TN), F32),
                        pltpu.VMEM((tm + 2 * FFN_HALO, FFN_TN), F32),
                        pltpu.VMEM((tm, d), F32)],
        compiler_params=_cparams("parallel"),
        name="conv_ffn",
    )(x, x, x, modtab, norm_w.reshape(1, d), w_up, conv_w, conv_b.reshape(1, 2 * dff), w_down,
      post_w.reshape(1, d))


def _rope_tables(n_tokens):
    t = jnp.arange(n_tokens)
    row = (t // GRID_W).astype(F32)
    col = (t % GRID_W).astype(F32)
    n_freq = HEAD_DIM // 4
    inv_freq = ROPE_THETA ** (-jnp.arange(n_freq, dtype=F32) / n_freq)
    ang = jnp.concatenate([row[:, None] * inv_freq, col[:, None] * inv_freq], axis=-1)
    cos, sin = jnp.cos(ang), jnp.sin(ang)
    cos_t = jnp.tile(cos, (1, LANES // (HEAD_DIM // 2)))
    sin_t = jnp.tile(jnp.concatenate([-sin, sin], axis=-1), (1, LANES // HEAD_DIM))
    return cos_t, sin_t


def _hgrn_tables():
    c = HGRN_CHUNK
    lower = jnp.tril(jnp.ones((c, c), F32))
    tri = jnp.stack([lower, lower.T])
    u = jnp.repeat(jnp.arange(HGRN_SUB), LANES)
    sel = (u[:, None] == (jnp.arange(c) % HGRN_SUB)[None, :]).astype(BF16)
    return tri, sel


def _lb_table(lb):
    pad = jnp.zeros((2, 5, lb.shape[-1]), F32)
    return jnp.concatenate([jnp.log(lb)[:, None], jnp.log1p(-lb)[:, None], (1.0 - lb)[:, None], pad], axis=1)


def kernel(x, c, ctx, c_ctx, w_mod, b_mod, norm_pre_mix, norm_post_mix, norm_pre_ffn, norm_post_ffn, even_w_in, even_w_out, even_q_norm, even_k_norm, hgrn_lb_logits, hgrn_out_norm, odd_w_qkv, odd_w_out, odd_rpb, ffn_w_up, ffn_conv_w, ffn_conv_b, ffn_w_down):
    bsz, seq, d = x.shape
    n_ctx = ctx.shape[1]
    depth = w_mod.shape[0]
    n_even = even_w_in.shape[0]

    modtab = _modulation(c, c_ctx, w_mod, b_mod)
    ctx_row = bsz
    rope = _rope_tables(seq)
    gmat = jnp.kron(jnp.eye(LANES // HEAD_DIM, dtype=F32), jnp.full((HEAD_DIM, HEAD_DIM), 1.0 / HEAD_DIM, F32))
    tri, sel = _hgrn_tables()
    p_lb = jax.nn.softmax(hgrn_lb_logits.astype(F32), axis=0)
    lb_all = jnp.concatenate([jnp.zeros_like(p_lb[:1]), jnp.cumsum(p_lb[1:], axis=0)], axis=0)

    xl = x.reshape(bsz * seq, d)
    xc = ctx.reshape(bsz * n_ctx, d)
    tm_lat = min(512, seq)
    tm_ctx = min(512, bsz * n_ctx)
    lat_kw = dict(seq_len=seq, mod_row=None, tm=tm_lat)
    ctx_kw = dict(seq_len=n_ctx, mod_row=ctx_row, tm=tm_ctx)

    for l in range(depth):
        with_ctx_out = l < depth - 1
        mt = modtab[l]
        if l % 2 == 0:
            e = l // 2
            w_in = even_w_in[e].astype(BF16)
            w_out = even_w_out[e].astype(BF16)
            qn = jnp.tile(even_q_norm[e], LANES // HEAD_DIM).reshape(1, LANES)
            kn = jnp.tile(even_k_norm[e], LANES // HEAD_DIM).reshape(1, LANES)
            pl_ = _even_in(xl, mt, norm_pre_mix[l], w_in, qn, kn, rope, gmat, **lat_kw)
            pc_ = _even_in(xc, mt, norm_pre_mix[l], w_in, qn, kn, None, gmat, **ctx_kw)
            qa_l, ka_l, va_l, bq_l, z_l, bi_l, sg_l = pl_
            qa_c, ka_c, va_c, bq_c, z_c, bi_c, sg_c = pc_
            r3 = lambda a, n: a.reshape(bsz, n, a.shape[-1])
            oa_l = _gqa(r3(qa_l, seq), r3(ka_c, n_ctx), r3(va_c, n_ctx), r3(ka_l, seq), r3(va_l, seq),
                        tq=min(GQA_TQ, seq))
            lbtab = _lb_table(lb_all[e])
            s0 = jnp.zeros((bsz, 2, B_HEADS, LANES, LANES), F32)
            ob_c, s_ctx = _hgrn(r3(bq_c, n_ctx), r3(z_c, n_ctx), r3(bi_c, n_ctx), lbtab, tri, sel, s0)
            ob_l, _ = _hgrn(r3(bq_l, seq), r3(z_l, seq), r3(bi_l, seq), lbtab, tri, sel, s_ctx)
            xl_new = _even_out(xl, mt, oa_l.reshape(bsz * seq, A_Q_W), ob_l.reshape(2, bsz * seq, B_W), sg_l,
                               hgrn_out_norm[e], w_out, norm_post_mix[l], **lat_kw)
            if with_ctx_out:
                oa_c = _gqa(r3(qa_c, n_ctx), r3(ka_c, n_ctx), r3(va_c, n_ctx), tq=n_ctx)
                xc = _even_out(xc, mt, oa_c.reshape(bsz * n_ctx, A_Q_W), ob_c.reshape(2, bsz * n_ctx, B_W), sg_c,
                               hgrn_out_norm[e], w_out, norm_post_mix[l], **ctx_kw)
            xl = xl_new
        else:
            o = l // 2
            w_qkv = odd_w_qkv[o].astype(BF16)
            w_out = odd_w_out[o].astype(BF16)
            ql, kl, vl = _odd_in(xl, mt, norm_pre_mix[l], w_qkv, **lat_kw)
            qc, kc, vc = _odd_in(xc, mt, norm_pre_mix[l], w_qkv, **ctx_kw)
            r3 = lambda a, n: a.reshape(bsz, n, a.shape[-1])
            o_l, o_c = _na(r3(ql, seq), r3(kl, seq), r3(vl, seq), r3(qc, n_ctx), r3(kc, n_ctx), r3(vc, n_ctx),
                           _na_bias(odd_rpb[o]))
            xl = _odd_out(xl, mt, o_l.reshape(bsz * seq, -1), w_out, norm_post_mix[l], **lat_kw)
            if with_ctx_out:
                xc = _odd_out(xc, mt, o_c.reshape(bsz * n_ctx, -1), w_out, norm_post_mix[l], **ctx_kw)

        w_up = ffn_w_up[l].astype(BF16)
        w_dn = ffn_w_down[l].astype(BF16)
        ffn_args = (norm_pre_ffn[l], w_up, ffn_conv_w[l], ffn_conv_b[l], w_dn, norm_post_ffn[l])
        xl = _ffn(xl, mt, *ffn_args, **lat_kw)
        if with_ctx_out:
            xc = _ffn(xc, mt, *ffn_args, **ctx_kw)
    return xl.reshape(bsz, seq, d)
```

```python
import functools

import numpy as np
import jax
import jax.numpy as jnp
from jax import lax
from jax.experimental import pallas as pl
from jax.experimental.pallas import tpu as pltpu

F32 = jnp.float32
BF16 = jnp.bfloat16
HIGHEST = lax.Precision.HIGHEST

V7X_VMEM_BYTES = 64 * 1024 * 1024
VMEM_LIMIT_BYTES = V7X_VMEM_BYTES - 8 * 1024 * 1024
LANES = 128
BF16_SUBLANES = 16

GRID_W = 64
HEAD_DIM = 64
A_Q_HEADS = 8
A_KV_HEADS = 2
A_GROUP = A_Q_HEADS // A_KV_HEADS
B_HEADS = 4
B_KEY_DIM = 128
C_HEADS = 16
NA_ROWS = 8
NA_COLS = 16
EPS = 1e-6
ROPE_THETA = 10000.0
ATTN_SCALE = HEAD_DIM ** -0.5
HGRN_SCALE = B_KEY_DIM ** -0.5
NEG_BIG = -1e30

A_Q_W = A_Q_HEADS * HEAD_DIM
A_KV_W = A_KV_HEADS * HEAD_DIM
B_W = B_HEADS * B_KEY_DIM

HGRN_CHUNK = 64
HGRN_SUB = 8
HGRN_BLOCK = 256
KV_BLOCK = 256
GQA_TQ = 256
GQA_SUB = 128
GQA_UNROLL = 4
NA_ROW_UNROLL = 8


def _cparams(*sem):
    return pltpu.CompilerParams(dimension_semantics=sem, vmem_limit_bytes=VMEM_LIMIT_BYTES)


def _const_spec(shape):
    n = len(shape)
    return pl.BlockSpec(shape, lambda *_: (0,) * n)


def _silu(x):
    return x / (1.0 + jnp.exp(-x))


def _rms(x, w):
    return x * lax.rsqrt(jnp.mean(x * x, axis=-1, keepdims=True) + EPS) * w


def _norm_mod(x, w, scale, shift):
    return _rms(x, w) * (1.0 + scale) + shift


def _mod_body(c_ref, w_ref, b_ref, o_ref):
    s = _silu(c_ref[...])
    o_ref[0] = jnp.dot(s, w_ref[0], preferred_element_type=F32, precision=HIGHEST) + b_ref[0]


def _modulation(c, c_ctx, w_mod, b_mod):
    depth, d, n = w_mod.shape
    b = c.shape[0]
    rows = -(-(b + 1) // 8) * 8
    cc = jnp.concatenate([c, c_ctx[None, :], jnp.zeros((rows - b - 1, d), F32)], axis=0)
    tn = n // 4
    out = pl.pallas_call(
        _mod_body,
        grid=(depth, n // tn),
        in_specs=[
            pl.BlockSpec((rows, d), lambda l, j: (0, 0)),
            pl.BlockSpec((1, d, tn), lambda l, j: (l, 0, j)),
            pl.BlockSpec((1, 1, tn), lambda l, j: (l, 0, j)),
        ],
        out_specs=pl.BlockSpec((1, rows, tn), lambda l, j: (l, 0, j)),
        out_shape=jax.ShapeDtypeStruct((depth, rows, n), F32),
        compiler_params=_cparams("parallel", "parallel"),
        name="adaln_mod",
    )(cc, w_mod, b_mod.reshape(depth, 1, n))
    return out.reshape(depth, rows, 6, d)


def _mod_index(seq_len, tm, mod_row):
    if mod_row is None:
        per_seq = seq_len // tm
        return lambda i: (i // per_seq, 0, 0)
    return lambda i: (mod_row, 0, 0)


def _even_in_body(x_ref, mod_ref, nw_ref, w_ref, qw_ref, kw_ref, cos_ref, sin_ref, gmat_ref,
                  qa_ref, ka_ref, va_ref, bq_ref, z_ref, bi_ref, sg_ref, *, use_rope):
    h = _norm_mod(x_ref[...], nw_ref[...], mod_ref[0, 1:2, :], mod_ref[0, 0:1, :]).astype(BF16)
    tm = h.shape[0]
    lane = lax.broadcasted_iota(jnp.int32, (tm, LANES), 1)
    lo_half = (lane % HEAD_DIM) < (HEAD_DIM // 2)
    lo_head = lane < HEAD_DIM
    gmat = gmat_ref[...]

    def proj(a, b):
        return jnp.dot(h, w_ref[:, a:b], preferred_element_type=F32)

    def head_norm_rope(zc, w):
        ms = jnp.dot(zc * zc, gmat, preferred_element_type=F32, precision=HIGHEST)
        y = zc * lax.rsqrt(ms + EPS) * w
        if use_rope:
            rot = jnp.where(lo_half, pltpu.roll(y, LANES - HEAD_DIM // 2, 1), pltpu.roll(y, HEAD_DIM // 2, 1))
            y = y * cos_ref[...] + rot * sin_ref[...]
        return y

    zq = proj(0, A_Q_W)
    for c in range(A_Q_W // LANES):
        y = head_norm_rope(zq[:, c * LANES:(c + 1) * LANES], qw_ref[...]) * ATTN_SCALE
        swapped = pltpu.roll(y, HEAD_DIM, 1)
        g = (2 * c) // A_GROUP
        if g == 0:
            h0, h1 = jnp.where(lo_head, y, 0.0), jnp.where(lo_head, swapped, 0.0)
        else:
            h0, h1 = jnp.where(lo_head, 0.0, swapped), jnp.where(lo_head, 0.0, y)
        qa_ref[:, (2 * c) * LANES:(2 * c + 1) * LANES] = h0.astype(BF16)
        qa_ref[:, (2 * c + 1) * LANES:(2 * c + 2) * LANES] = h1.astype(BF16)
    o = A_Q_W
    ka_ref[...] = head_norm_rope(proj(o, o + A_KV_W), kw_ref[...]).astype(BF16)
    o += A_KV_W
    va_ref[...] = proj(o, o + A_KV_W).astype(BF16)
    o += A_KV_W
    bq_ref[...] = (_silu(proj(o, o + B_W)) * HGRN_SCALE).astype(BF16)
    o += B_W
    z_ref[...] = proj(o, o + 2 * B_W)
    o += 2 * B_W
    bi_ref[...] = proj(o, o + B_W).astype(BF16)
    o += B_W
    sg_ref[...] = _silu(proj(o, o + B_W)).astype(BF16)


def _even_in(x, modtab, norm_w, w_in, q_norm, k_norm, rope, gmat, *, seq_len, mod_row, tm):
    n, d = x.shape
    use_rope = rope is not None
    cos, sin = rope if use_rope else (jnp.zeros((8, LANES), F32),) * 2
    per_seq = seq_len // tm
    rope_spec = pl.BlockSpec((tm, LANES), lambda i: (i % per_seq, 0)) if use_rope else _const_spec((8, LANES))
    row = lambda w: pl.BlockSpec((tm, w), lambda i: (i, 0))
    widths = (2 * A_Q_W, A_KV_W, A_KV_W, B_W, 2 * B_W, B_W, B_W)
    dtypes = (BF16, BF16, BF16, BF16, F32, BF16, BF16)
    return pl.pallas_call(
        functools.partial(_even_in_body, use_rope=use_rope),
        grid=(n // tm,),
        in_specs=[
            row(d),
            pl.BlockSpec((1, 6, d), _mod_index(seq_len, tm, mod_row)),
            _const_spec((1, d)),
            _const_spec(w_in.shape),
            _const_spec((1, LANES)),
            _const_spec((1, LANES)),
            rope_spec, rope_spec,
            _const_spec((LANES, LANES)),
        ],
        out_specs=[row(w) for w in widths],
        out_shape=[jax.ShapeDtypeStruct((n, w), t) for w, t in zip(widths, dtypes)],
        compiler_params=_cparams("parallel"),
        name="even_in",
    )(x, modtab, norm_w.reshape(1, d), w_in, q_norm, k_norm, cos, sin, gmat)


def _gqa_body(*refs, tq, n_lat_blk):
    if n_lat_blk:
        q_ref, kc_ref, vc_ref, kl_ref, vl_ref, o_ref, s_ref, m_ref, acc_ref = refs
    else:
        q_ref, kc_ref, vc_ref, o_ref, s_ref, m_ref, acc_ref = refs
    sq = GQA_SUB
    lo_head = lax.broadcasted_iota(jnp.int32, (sq, LANES), 1) < HEAD_DIM
    lo_head_kv = lax.broadcasted_iota(jnp.int32, (KV_BLOCK, LANES), 1) < HEAD_DIM
    units = [(g, sub) for g in range(A_KV_HEADS) for sub in range(tq // sq)]

    def lat_rows(j):
        return pl.ds(pl.multiple_of(j * KV_BLOCK, KV_BLOCK), KV_BLOCK)

    def load_q(g, sub):
        return jnp.concatenate(
            [q_ref[0, sub * sq:(sub + 1) * sq, (A_GROUP * g + j) * LANES:(A_GROUP * g + j + 1) * LANES]
             for j in range(A_GROUP)], axis=0)

    def scores(slot, q, j, k):
        s = lax.dot_general(q, k, (((1,), (1,)), ((), ())), preferred_element_type=F32)
        s_ref[slot, j] = s
        blk_max = s[:, 0:LANES]
        for c in range(1, KV_BLOCK // LANES):
            blk_max = jnp.maximum(blk_max, s[:, c * LANES:(c + 1) * LANES])
        return blk_max

    def probs(slot, j):
        m = m_ref[slot]
        return jnp.exp(s_ref[slot, j] - jnp.concatenate([m] * (KV_BLOCK // LANES), axis=1)).astype(BF16)

    def weighted(g, p, v):
        own = lo_head_kv if g == 0 else ~lo_head_kv
        own = jnp.concatenate([own] * (v.shape[0] // KV_BLOCK), axis=0)
        return jnp.dot(p, jnp.where(own, v, jnp.ones_like(v)), preferred_element_type=F32)

    def finish(g, sub):
        acc = acc_ref[...]
        o = acc / pltpu.roll(acc, HEAD_DIM, 1)
        for c in range(A_GROUP // 2):
            o0 = o[(2 * c) * sq:(2 * c + 1) * sq]
            o1 = o[(2 * c + 1) * sq:(2 * c + 2) * sq]
            if g == 0:
                pair = jnp.where(lo_head, o0, pltpu.roll(o1, HEAD_DIM, 1))
            else:
                pair = jnp.where(lo_head, pltpu.roll(o0, HEAD_DIM, 1), o1)
            col = (A_GROUP // 2 * g + c) * LANES
            o_ref[0, sub * sq:(sub + 1) * sq, col:col + LANES] = pair.astype(BF16)

    def phase(idx):
        score = units[idx] if idx < len(units) else None
        pv = units[idx - 1] if idx >= 1 else None
        slot_s, slot_p = idx % 2, (idx - 1) % 2
        if score:
            m_ref[slot_s] = scores(slot_s, load_q(*score), 0, kc_ref[0])
        if pv:
            acc_ref[...] = weighted(pv[0], probs(slot_p, 0), vc_ref[0])
        if n_lat_blk:
            def step(i, carry):
                blk_max, ps = None, []
                for u in range(GQA_UNROLL):
                    j = i * GQA_UNROLL + u
                    if score:
                        cur = scores(slot_s, load_q(*score), j + 1, kl_ref[0, lat_rows(j), :])
                        blk_max = cur if blk_max is None else jnp.maximum(blk_max, cur)
                    if pv:
                        ps.append(probs(slot_p, j + 1))
                if score:
                    m_ref[slot_s] = jnp.maximum(m_ref[slot_s], blk_max)
                if pv:
                    span = GQA_UNROLL * KV_BLOCK
                    v = vl_ref[0, pl.ds(pl.multiple_of(i * span, span), span), :]
                    acc_ref[...] += weighted(pv[0], jnp.concatenate(ps, axis=1), v)
                return carry
            lax.fori_loop(0, n_lat_blk // GQA_UNROLL, step, 0)
        if score:
            m_ref[slot_s] = jnp.broadcast_to(jnp.max(m_ref[slot_s], axis=1, keepdims=True), m_ref.shape[1:])
        if pv:
            finish(*pv)

    for idx in range(len(units) + 1):
        phase(idx)


def _gqa(q, k_ctx, v_ctx, k_lat=None, v_lat=None, *, tq):
    b, nq, _ = q.shape
    n_ctx = k_ctx.shape[1]
    assert n_ctx == KV_BLOCK
    with_lat = k_lat is not None
    n_lat = k_lat.shape[1] if with_lat else 0
    kv_spec = lambda n: pl.BlockSpec((1, n, A_KV_W), lambda i, j: (i, 0, 0))
    in_specs = [pl.BlockSpec((1, tq, 2 * A_Q_W), lambda i, j: (i, j, 0)), kv_spec(n_ctx), kv_spec(n_ctx)]
    args = [q, k_ctx, v_ctx]
    if with_lat:
        in_specs += [kv_spec(n_lat), kv_spec(n_lat)]
        args += [k_lat, v_lat]
    rows = A_GROUP * GQA_SUB
    n_blk = 1 + n_lat // KV_BLOCK
    assert (n_blk - 1) % GQA_UNROLL == 0 and tq % GQA_SUB == 0
    return pl.pallas_call(
        functools.partial(_gqa_body, tq=tq, n_lat_blk=n_lat // KV_BLOCK),
        grid=(b, nq // tq),
        in_specs=in_specs,
        out_specs=pl.BlockSpec((1, tq, A_Q_W), lambda i, j: (i, j, 0)),
        out_shape=jax.ShapeDtypeStruct((b, nq, A_Q_W), BF16),
        scratch_shapes=[pltpu.VMEM((2, n_blk, rows, KV_BLOCK), F32), pltpu.VMEM((2, rows, LANES), F32),
                        pltpu.VMEM((rows, LANES), F32)],
        compiler_params=_cparams("parallel", "parallel"),
        name="gqa_lat" if with_lat else "gqa_ctx",
    )(*args)


def _hgrn_masks(rev):
    c = HGRN_CHUNK
    t = lax.broadcasted_iota(jnp.int32, (c, c), 0)
    s = lax.broadcasted_iota(jnp.int32, (c, c), 1)
    causal = (s >= t) if rev else (s <= t)
    diag = ((t // HGRN_SUB) == (s // HGRN_SUB)) & causal
    levels = []
    size = 2 * HGRN_SUB
    while size <= c:
        half = size // 2
        t_hi, s_hi = (t % size) >= half, (s % size) >= half
        pair = (~t_hi & s_hi) if rev else (t_hi & ~s_hi)
        levels.append((size, ((t // size) == (s // size)) & pair))
        size *= 2
    return diag, levels


def _hgrn_chunk(row0, rev, q_ref, z_ref, v_ref, lb_ref, tri_ref, sel_ref, o_ref, st_ref, kbuf, bbuf, pcat):
    c, sub = HGRN_CHUNK, HGRN_SUB
    rows = pl.ds(row0, c)
    diag_mask, level_masks = _hgrn_masks(rev)
    tri = tri_ref[...]
    per_head = []
    for hd in range(B_HEADS):
        cols = slice(hd * LANES, (hd + 1) * LANES)
        q = q_ref[0, rows, cols].astype(F32)
        z = z_ref[0, rows, cols]
        log_lb, log_1m_lb, one_m_lb = lb_ref[0, 0:1, cols], lb_ref[0, 1:2, cols], lb_ref[0, 2:3, cols]
        log_sig = jnp.minimum(z, 0.0) - jnp.log(1.0 + jnp.exp(-jnp.abs(z)))
        kg = one_m_lb * jnp.exp(log_sig - z)
        cterm = log_1m_lb + log_sig
        log_f = jnp.maximum(log_lb, cterm) + jnp.log(1.0 + jnp.exp(-jnp.abs(log_lb - cterm)))
        b = jnp.dot(tri, log_f, preferred_element_type=F32, precision=HIGHEST)
        kbuf[hd] = kg
        bbuf[hd] = b
        for u in range(sub):
            ku = jnp.concatenate(
                [jnp.broadcast_to(kbuf[hd, m * sub + u:m * sub + u + 1, :], (sub, LANES)) for m in range(c // sub)], 0)
            bu = jnp.concatenate(
                [jnp.broadcast_to(bbuf[hd, m * sub + u:m * sub + u + 1, :], (sub, LANES)) for m in range(c // sub)], 0)
            pcat[hd * c:(hd + 1) * c, u * LANES:(u + 1) * LANES] = (
                q * ku * jnp.exp(jnp.minimum(b - bu, 0.0))).astype(BF16)
        per_head.append((q, kg, b))
    a_diag = jnp.dot(pcat[...], sel_ref[...], preferred_element_type=F32)
    for hd in range(B_HEADS):
        cols = slice(hd * LANES, (hd + 1) * LANES)
        q, kg, b = per_head[hd]
        v = v_ref[0, rows, cols]
        a = jnp.where(diag_mask, a_diag[hd * c:(hd + 1) * c], 0.0)
        for size, mask in level_masks:
            half = size // 2
            ref_rows = [m * size + (half if rev else half - 1) for m in range(c // size)]
            beta = jnp.concatenate(
                [jnp.broadcast_to(bbuf[hd, r:r + 1, :], (size, LANES)) for r in ref_rows], 0)
            e = jnp.exp(-jnp.abs(b - beta))
            lvl = lax.dot_general((q * e).astype(BF16), (kg * e).astype(BF16), (((1,), (1,)), ((), ())),
                                  preferred_element_type=F32)
            a = a + jnp.where(mask, lvl, 0.0)
        b_end = bbuf[hd, 0:1, :] if rev else bbuf[hd, c - 1:c, :]
        st = st_ref[0, 0, hd]
        o = jnp.dot(a.astype(BF16), v, preferred_element_type=F32)
        o = o + lax.dot_general((q * jnp.exp(b)).astype(BF16), st.astype(BF16), (((1,), (1,)), ((), ())),
                                preferred_element_type=F32)
        o_ref[0, 0, rows, cols] = o
        kd = (kg * jnp.exp(b_end - b)).astype(BF16)
        upd = lax.dot_general(v, kd, (((0,), (0,)), ((), ())), preferred_element_type=F32)
        st_ref[0, 0, hd] = st * jnp.exp(b_end) + upd


def _hgrn_body(q_ref, z_ref, v_ref, lb_ref, tri_ref, sel_ref, s0_ref, o_ref, st_ref, kbuf, bbuf, pcat):
    d, j = pl.program_id(1), pl.program_id(2)
    n_chunks = HGRN_BLOCK // HGRN_CHUNK

    @pl.when(j == 0)
    def _():
        st_ref[...] = s0_ref[...]

    for rev in (False, True):
        @pl.when(d == int(rev))
        def _(rev=rev):
            for i in range(n_chunks):
                ci = (n_chunks - 1 - i) if rev else i
                _hgrn_chunk(ci * HGRN_CHUNK, rev, q_ref, z_ref, v_ref, lb_ref, tri_ref, sel_ref, o_ref, st_ref,
                            kbuf.at[ci], bbuf.at[ci], pcat.at[ci])


def _hgrn(q, z, v, lbtab, tri, sel, s0):
    b, n, _ = q.shape
    nblk = n // HGRN_BLOCK
    c = HGRN_CHUNK

    def blk(d, j):
        return jnp.where(d == 0, j, nblk - 1 - j)

    tok = lambda i, d, j: (i, blk(d, j), 0)
    st_spec = pl.BlockSpec((1, 1, B_HEADS, LANES, LANES), lambda i, d, j: (i, d, 0, 0, 0))
    return pl.pallas_call(
        _hgrn_body,
        grid=(b, 2, nblk),
        in_specs=[
            pl.BlockSpec((1, HGRN_BLOCK, B_W), tok),
            pl.BlockSpec((1, HGRN_BLOCK, B_W), lambda i, d, j: (i, blk(d, j), d)),
            pl.BlockSpec((1, HGRN_BLOCK, B_W), tok),
            pl.BlockSpec((1, 8, B_W), lambda i, d, j: (d, 0, 0)),
            pl.BlockSpec((None, c, c), lambda i, d, j: (d, 0, 0)),
            _const_spec(sel.shape),
            st_spec,
        ],
        out_specs=[
            pl.BlockSpec((1, 1, HGRN_BLOCK, B_W), lambda i, d, j: (d, i, blk(d, j), 0)),
            st_spec,
        ],
        out_shape=[jax.ShapeDtypeStruct((2, b, n, B_W), F32),
                   jax.ShapeDtypeStruct(s0.shape, F32)],
        scratch_shapes=[pltpu.VMEM((HGRN_BLOCK // c, B_HEADS, c, LANES), F32),
                        pltpu.VMEM((HGRN_BLOCK // c, B_HEADS, c, LANES), F32),
                        pltpu.VMEM((HGRN_BLOCK // c, B_HEADS * c, HGRN_SUB * LANES), BF16)],
        compiler_params=_cparams("parallel", "parallel", "arbitrary"),
        name="hgrn_scan",
    )(q, z, v, lbtab, tri, sel, s0)


def _odd_in_body(x_ref, mod_ref, nw_ref, w_ref, q_ref, k_ref, v_ref):
    h = _norm_mod(x_ref[...], nw_ref[...], mod_ref[0, 1:2, :], mod_ref[0, 0:1, :]).astype(BF16)
    cw = q_ref.shape[1]
    q_ref[...] = (jnp.dot(h, w_ref[:, 0:cw], preferred_element_type=F32) * ATTN_SCALE).astype(BF16)
    k_ref[...] = jnp.dot(h, w_ref[:, cw:2 * cw], preferred_element_type=F32).astype(BF16)
    v_ref[...] = jnp.dot(h, w_ref[:, 2 * cw:3 * cw], preferred_element_type=F32).astype(BF16)


def _odd_in(x, modtab, norm_w, w_qkv, *, seq_len, mod_row, tm):
    n, d = x.shape
    cw = w_qkv.shape[1] // 3
    row = pl.BlockSpec((tm, cw), lambda i: (i, 0))
    return pl.pallas_call(
        _odd_in_body,
        grid=(n // tm,),
        in_specs=[
            pl.BlockSpec((tm, d), lambda i: (i, 0)),
            pl.BlockSpec((1, 6, d), _mod_index(seq_len, tm, mod_row)),
            _const_spec((1, d)),
            _const_spec(w_qkv.shape),
        ],
        out_specs=[row, row, row],
        out_shape=[jax.ShapeDtypeStruct((n, cw), BF16)] * 3,
        compiler_params=_cparams("parallel"),
        name="odd_in",
    )(x, modtab, norm_w.reshape(1, d), w_qkv)


def _na_body(ql_ref, kl_ref, vl_ref, qc_ref, kc_ref, vc_ref, bias_ref, ol_ref, oc_ref, *, rows):
    band = NA_ROWS * GRID_W
    lane_q = lax.broadcasted_iota(jnp.int32, (GRID_W, LANES), 1) < HEAD_DIM
    kc, vc = kc_ref[0], vc_ref[0]
    nt = (((1,), (1,)), ((), ()))

    def two_heads(q, lo_mask):
        zero = jnp.zeros_like(q)
        return jnp.concatenate([jnp.where(lo_mask, q, zero), jnp.where(lo_mask, zero, q)], axis=0)

    def merge(o, n, lo_mask):
        return jnp.where(lo_mask, o[:n], o[n:])

    def with_ones(v):
        return jnp.concatenate([v, jnp.ones_like(v)], axis=1)

    vc_aug = with_ones(vc)

    def row_group(i, carry):
        n2 = 2 * GRID_W
        rs = [i * NA_ROW_UNROLL + u for u in range(NA_ROW_UNROLL)]
        q = jnp.concatenate(
            [two_heads(ql_ref[0, pl.ds(pl.multiple_of(r * GRID_W, GRID_W), GRID_W), :], lane_q) for r in rs], axis=0)
        s_ctx = lax.dot_general(q, kc, nt, preferred_element_type=F32)
        p_ctx, p_loc, v_band = [], [], []
        for u, r in enumerate(rs):
            r0 = jnp.clip(r - NA_ROWS // 2, 0, rows - NA_ROWS)
            kstart = pl.multiple_of(r0 * GRID_W, GRID_W)
            s_loc = lax.dot_general(q[u * n2:(u + 1) * n2], kl_ref[0, pl.ds(kstart, band), :], nt,
                                    preferred_element_type=F32) + bias_ref[0, r - r0]
            s_c = s_ctx[u * n2:(u + 1) * n2]
            m = jnp.maximum(jnp.max(s_loc, axis=1, keepdims=True), jnp.max(s_c, axis=1, keepdims=True))
            p_loc.append(jnp.exp(s_loc - m).astype(BF16))
            p_ctx.append(jnp.exp(s_c - m).astype(BF16))
            v_band.append(vl_ref[0, pl.ds(kstart, band), :])
        o_ctx = jnp.dot(jnp.concatenate(p_ctx, axis=0), vc_aug, preferred_element_type=F32)
        for u, r in enumerate(rs):
            o = jnp.dot(p_loc[u], with_ones(v_band[u]), preferred_element_type=F32) + o_ctx[u * n2:(u + 1) * n2]
            o = o[:, :LANES] / o[:, LANES:]
            ol_ref[0, pl.ds(pl.multiple_of(r * GRID_W, GRID_W), GRID_W), :] = (
                merge(o, GRID_W, lane_q).astype(BF16))
        return carry

    lax.fori_loop(0, rows // NA_ROW_UNROLL, row_group, 0)

    n_ctx = qc_ref.shape[1]
    lane_c = lax.broadcasted_iota(jnp.int32, (n_ctx, LANES), 1) < HEAD_DIM
    q = two_heads(qc_ref[0], lane_c)
    s = lax.dot_general(q, kc, nt, preferred_element_type=F32)
    p = jnp.exp(s - jnp.max(s, axis=1, keepdims=True)).astype(BF16)
    o = jnp.dot(p, vc_aug, preferred_element_type=F32)
    oc_ref[0] = merge(o[:, :LANES] / o[:, LANES:], n_ctx, lane_c).astype(BF16)


def _na_bias(rpb):
    heads, n_dy, n_dx = rpb.shape
    qc = np.arange(GRID_W)
    kc = np.arange(GRID_W)
    c_start = np.clip(qc - NA_COLS // 2, 0, GRID_W - NA_COLS)
    in_win = (kc[None, :] >= c_start[:, None]) & (kc[None, :] < c_start[:, None] + NA_COLS)
    dx = kc[None, :] - qc[:, None] + (NA_COLS - 1)
    onehot = ((dx[:, :, None] == np.arange(n_dx)[None, None, :]) & in_win[:, :, None]).astype(np.float32)
    t = jnp.einsum("hyx,qcx->hqyc", rpb.astype(F32), jnp.asarray(onehot), precision=HIGHEST)
    t = jnp.where(jnp.asarray(in_win)[None, :, None, :], t, NEG_BIG).reshape(heads, GRID_W, n_dy * GRID_W)
    width = NA_ROWS * GRID_W
    tab = jnp.stack([t[:, :, (NA_ROWS - 1 - var) * GRID_W:(NA_ROWS - 1 - var) * GRID_W + width]
                     for var in range(NA_ROWS)], axis=1)
    tab = tab.reshape(heads // 2, 2, NA_ROWS, GRID_W, width)
    return tab.transpose(0, 2, 1, 3, 4).reshape(heads // 2, NA_ROWS, 2 * GRID_W, width)


def _na(q_lat, k_lat, v_lat, q_ctx, k_ctx, v_ctx, bias):
    b, n, w = q_lat.shape
    n_ctx = q_ctx.shape[1]
    rows = n // GRID_W
    assert rows >= NA_ROWS and rows % NA_ROW_UNROLL == 0
    pairs = w // LANES
    lat = pl.BlockSpec((1, n, LANES), lambda p, i: (i, 0, p))
    ctx = pl.BlockSpec((1, n_ctx, LANES), lambda p, i: (i, 0, p))
    return pl.pallas_call(
        functools.partial(_na_body, rows=rows),
        grid=(pairs, b),
        in_specs=[lat, lat, lat, ctx, ctx, ctx,
                  pl.BlockSpec((1,) + bias.shape[1:], lambda p, i: (p, 0, 0, 0))],
        out_specs=[lat, ctx],
        out_shape=[jax.ShapeDtypeStruct((b, n, w), BF16), jax.ShapeDtypeStruct((b, n_ctx, w), BF16)],
        compiler_params=_cparams("parallel", "parallel"),
        name="na_attn",
    )(q_lat, k_lat, v_lat, q_ctx, k_ctx, v_ctx, bias)


def _post(acc, post_w, gate, x):
    return x + gate * _rms(acc, post_w)


def _even_out_body(x_ref, mod_ref, oa_ref, ob_ref, sg_ref, onw_ref, w_ref, pw_ref, y_ref):
    acc = jnp.dot(oa_ref[...], w_ref[0:A_Q_W, :], preferred_element_type=F32)
    ob = ob_ref[0] + ob_ref[1]
    parts = []
    for hd in range(B_HEADS):
        cols = slice(hd * LANES, (hd + 1) * LANES)
        parts.append((_rms(ob[:, cols], onw_ref[...]) * sg_ref[:, cols].astype(F32)).astype(BF16))
    acc = acc + jnp.dot(jnp.concatenate(parts, axis=1), w_ref[A_Q_W:, :], preferred_element_type=F32)
    y_ref[...] = _post(acc, pw_ref[...], mod_ref[0, 2:3, :], x_ref[...])


def _even_out(x, modtab, oa, ob, sg, out_norm, w_out, post_w, *, seq_len, mod_row, tm):
    n, d = x.shape
    row = lambda w: pl.BlockSpec((tm, w), lambda i: (i, 0))
    return pl.pallas_call(
        _even_out_body,
        grid=(n // tm,),
        in_specs=[
            row(d),
            pl.BlockSpec((1, 6, d), _mod_index(seq_len, tm, mod_row)),
            row(A_Q_W),
            pl.BlockSpec((2, tm, B_W), lambda i: (0, i, 0)),
            row(B_W),
            _const_spec((1, LANES)),
            _const_spec(w_out.shape),
            _const_spec((1, d)),
        ],
        out_specs=row(d),
        out_shape=jax.ShapeDtypeStruct((n, d), F32),
        compiler_params=_cparams("parallel"),
        name="even_out",
    )(x, modtab, oa, ob, sg, out_norm.reshape(1, LANES), w_out, post_w.reshape(1, d))


def _odd_out_body(x_ref, mod_ref, o_ref, w_ref, pw_ref, y_ref):
    acc = jnp.dot(o_ref[...], w_ref[...], preferred_element_type=F32)
    y_ref[...] = _post(acc, pw_ref[...], mod_ref[0, 2:3, :], x_ref[...])


def _odd_out(x, modtab, o, w_out, post_w, *, seq_len, mod_row, tm):
    n, d = x.shape
    row = lambda w: pl.BlockSpec((tm, w), lambda i: (i, 0))
    return pl.pallas_call(
        _odd_out_body,
        grid=(n // tm,),
        in_specs=[
            row(d),
            pl.BlockSpec((1, 6, d), _mod_index(seq_len, tm, mod_row)),
            row(o.shape[1]),
            _const_spec(w_out.shape),
            _const_spec((1, d)),
        ],
        out_specs=row(d),
        out_shape=jax.ShapeDtypeStruct((n, d), F32),
        compiler_params=_cparams("parallel"),
        name="odd_out",
    )(x, modtab, o, w_out, post_w.reshape(1, d))


FFN_HALO = BF16_SUBLANES
FFN_TN = 256


def _ffn_body(x_ref, xp_ref, xn_ref, mod_ref, nw_ref, wup_ref, cw_ref, cb_ref, wdn_ref, pw_ref, y_ref,
              hbuf, ua, ug, act, *, seq_len):
    tm, d = x_ref.shape
    dff = wdn_ref.shape[0]
    halo = FFN_HALO
    scale, shift = mod_ref[0, 4:5, :], mod_ref[0, 3:4, :]
    x = x_ref[...]
    row0 = pl.program_id(0) * tm
    top_in = (row0 % seq_len) != 0
    bottom_in = ((row0 + tm) % seq_len) != 0
    hbuf[0:halo, :] = jnp.where(top_in, _norm_mod(xp_ref[...], nw_ref[...], scale, shift), 0.0).astype(BF16)
    hbuf[halo:halo + tm, :] = _norm_mod(x, nw_ref[...], scale, shift).astype(BF16)
    hbuf[halo + tm:, :] = jnp.where(bottom_in, _norm_mod(xn_ref[...], nw_ref[...], scale, shift), 0.0).astype(BF16)
    h = hbuf[...]
    for j in range(dff // FFN_TN):
        halves = []
        for buf, base in ((ua, 0), (ug, dff)):
            cols = slice(base + j * FFN_TN, base + (j + 1) * FFN_TN)
            buf[...] = jnp.dot(h, wup_ref[:, cols], preferred_element_type=F32)
            halves.append(
                cw_ref[0:1, cols] * buf[halo - 1:halo - 1 + tm, :]
                + cw_ref[1:2, cols] * buf[halo:halo + tm, :]
                + cw_ref[2:3, cols] * buf[halo + 1:halo + 1 + tm, :]
                + cb_ref[0:1, cols])
        a, g = halves
        act[:, j * FFN_TN:(j + 1) * FFN_TN] = (_silu(g) * a).astype(BF16)
    out = jnp.dot(act[...], wdn_ref[...], preferred_element_type=F32)
    y_ref[...] = _post(out, pw_ref[...], mod_ref[0, 5:6, :], x)


def _ffn(x, modtab, norm_w, w_up, conv_w, conv_b, w_down, post_w, *, seq_len, mod_row, tm):
    n, d = x.shape
    dff = w_down.shape[0]
    assert dff % FFN_TN == 0 and tm % FFN_HALO == 0 and seq_len % tm == 0
    per_tile = tm // FFN_HALO
    last = n // FFN_HALO - 1
    return pl.pallas_call(
        functools.partial(_ffn_body, seq_len=seq_len),
        grid=(n // tm,),
        in_specs=[
            pl.BlockSpec((tm, d), lambda i: (i, 0)),
            pl.BlockSpec((FFN_HALO, d), lambda i: (jnp.maximum(i * per_tile - 1, 0), 0)),
            pl.BlockSpec((FFN_HALO, d), lambda i: (jnp.minimum((i + 1) * per_tile, last), 0)),
            pl.BlockSpec((1, 6, d), _mod_index(seq_len, tm, mod_row)),
            _const_spec((1, d)),
            _const_spec(w_up.shape),
            _const_spec(conv_w.shape),
            _const_spec((1, 2 * dff)),
            _const_spec(w_down.shape),
            _const_spec((1, d)),
        ],
        out_specs=pl.BlockSpec((tm, d), lambda i: (i, 0)),
        out_shape=jax.ShapeDtypeStruct((n, d), F32),
        scratch_shapes=[pltpu.VMEM((tm + 2 * FFN_HALO, d), BF16),
                        pltpu.VMEM((tm + 2 * FFN_HALO, FFN_TN), F32),
                        pltpu.VMEM((tm + 2 * FFN_HALO, FFN_TN), F32),
                        pltpu.VMEM((tm, dff), BF16)],
        compiler_params=_cparams("parallel"),
        name="conv_ffn",
    )(x, x, x, modtab, norm_w.reshape(1, d), w_up, conv_w, conv_b.reshape(1, 2 * dff), w_down,
      post_w.reshape(1, d))


def _rope_tables(n_tokens):
    t = jnp.arange(n_tokens)
    row = (t // GRID_W).astype(F32)
    col = (t % GRID_W).astype(F32)
    n_freq = HEAD_DIM // 4
    inv_freq = ROPE_THETA ** (-jnp.arange(n_freq, dtype=F32) / n_freq)
    ang = jnp.concatenate([row[:, None] * inv_freq, col[:, None] * inv_freq], axis=-1)
    cos, sin = jnp.cos(ang), jnp.sin(ang)
    cos_t = jnp.tile(cos, (1, LANES // (HEAD_DIM // 2)))
    sin_t = jnp.tile(jnp.concatenate([-sin, sin], axis=-1), (1, LANES // HEAD_DIM))
    return cos_t, sin_t


def _hgrn_tables():
    c = HGRN_CHUNK
    lower = jnp.tril(jnp.ones((c, c), F32))
    tri = jnp.stack([lower, lower.T])
    u = jnp.repeat(jnp.arange(HGRN_SUB), LANES)
    sel = (u[:, None] == (jnp.arange(c) % HGRN_SUB)[None, :]).astype(BF16)
    return tri, sel


def _lb_table(lb):
    pad = jnp.zeros((2, 5, lb.shape[-1]), F32)
    return jnp.concatenate([jnp.log(lb)[:, None], jnp.log1p(-lb)[:, None], (1.0 - lb)[:, None], pad], axis=1)


def kernel(x, c, ctx, c_ctx, w_mod, b_mod, norm_pre_mix, norm_post_mix, norm_pre_ffn, norm_post_ffn, even_w_in, even_w_out, even_q_norm, even_k_norm, hgrn_lb_logits, hgrn_out_norm, odd_w_qkv, odd_w_out, odd_rpb, ffn_w_up, ffn_conv_w, ffn_conv_b, ffn_w_down):
    bsz, seq, d = x.shape
    n_ctx = ctx.shape[1]
    depth = w_mod.shape[0]
    n_even = even_w_in.shape[0]

    modtab = _modulation(c, c_ctx, w_mod, b_mod)
    ctx_row = bsz
    rope = _rope_tables(seq)
    gmat = jnp.kron(jnp.eye(LANES // HEAD_DIM, dtype=F32), jnp.full((HEAD_DIM, HEAD_DIM), 1.0 / HEAD_DIM, F32))
    tri, sel = _hgrn_tables()
    p_lb = jax.nn.softmax(hgrn_lb_logits.astype(F32), axis=0)
    lb_all = jnp.concatenate([jnp.zeros_like(p_lb[:1]), jnp.cumsum(p_lb[1:], axis=0)], axis=0)

    xl = x.reshape(bsz * seq, d)
    xc = ctx.reshape(bsz * n_ctx, d)
    tm_lat = min(512, seq)
    tm_ctx = min(512, bsz * n_ctx)
    lat_kw = dict(seq_len=seq, mod_row=None, tm=tm_lat)
    ctx_kw = dict(seq_len=n_ctx, mod_row=ctx_row, tm=tm_ctx)

    for l in range(depth):
        with_ctx_out = l < depth - 1
        mt = modtab[l]
        if l % 2 == 0:
            e = l // 2
            w_in = even_w_in[e].astype(BF16)
            w_out = even_w_out[e].astype(BF16)
            qn = jnp.tile(even_q_norm[e], LANES // HEAD_DIM).reshape(1, LANES)
            kn = jnp.tile(even_k_norm[e], LANES // HEAD_DIM).reshape(1, LANES)
            pl_ = _even_in(xl, mt, norm_pre_mix[l], w_in, qn, kn, rope, gmat, **lat_kw)
            pc_ = _even_in(xc, mt, norm_pre_mix[l], w_in, qn, kn, None, gmat, **ctx_kw)
            qa_l, ka_l, va_l, bq_l, z_l, bi_l, sg_l = pl_
            qa_c, ka_c, va_c, bq_c, z_c, bi_c, sg_c = pc_
            r3 = lambda a, n: a.reshape(bsz, n, a.shape[-1])
            oa_l = _gqa(r3(qa_l, seq), r3(ka_c, n_ctx), r3(va_c, n_ctx), r3(ka_l, seq), r3(va_l, seq),
                        tq=min(GQA_TQ, seq))
            lbtab = _lb_table(lb_all[e])
            s0 = jnp.zeros((bsz, 2, B_HEADS, LANES, LANES), F32)
            ob_c, s_ctx = _hgrn(r3(bq_c, n_ctx), r3(z_c, n_ctx), r3(bi_c, n_ctx), lbtab, tri, sel, s0)
            ob_l, _ = _hgrn(r3(bq_l, seq), r3(z_l, seq), r3(bi_l, seq), lbtab, tri, sel, s_ctx)
            xl_new = _even_out(xl, mt, oa_l.reshape(bsz * seq, A_Q_W), ob_l.reshape(2, bsz * seq, B_W), sg_l,
                               hgrn_out_norm[e], w_out, norm_post_mix[l], **lat_kw)
            if with_ctx_out:
                oa_c = _gqa(r3(qa_c, n_ctx), r3(ka_c, n_ctx), r3(va_c, n_ctx), tq=n_ctx)
                xc = _even_out(xc, mt, oa_c.reshape(bsz * n_ctx, A_Q_W), ob_c.reshape(2, bsz * n_ctx, B_W), sg_c,
                               hgrn_out_norm[e], w_out, norm_post_mix[l], **ctx_kw)
            xl = xl_new
        else:
            o = l // 2
            w_qkv = odd_w_qkv[o].astype(BF16)
            w_out = odd_w_out[o].astype(BF16)
            ql, kl, vl = _odd_in(xl, mt, norm_pre_mix[l], w_qkv, **lat_kw)
            qc, kc, vc = _odd_in(xc, mt, norm_pre_mix[l], w_qkv, **ctx_kw)
            r3 = lambda a, n: a.reshape(bsz, n, a.shape[-1])
            o_l, o_c = _na(r3(ql, seq), r3(kl, seq), r3(vl, seq), r3(qc, n_ctx), r3(kc, n_ctx), r3(vc, n_ctx),
                           _na_bias(odd_rpb[o]))
            xl = _odd_out(xl, mt, o_l.reshape(bsz * seq, -1), w_out, norm_post_mix[l], **lat_kw)
            if with_ctx_out:
                xc = _odd_out(xc, mt, o_c.reshape(bsz * n_ctx, -1), w_out, norm_post_mix[l], **ctx_kw)

        w_up = ffn_w_up[l].astype(BF16)
        w_dn = ffn_w_down[l].astype(BF16)
        ffn_args = (norm_pre_ffn[l], w_up, ffn_conv_w[l], ffn_conv_b[l], w_dn, norm_post_ffn[l])
        xl = _ffn(xl, mt, *ffn_args, **lat_kw)
        if with_ctx_out:
            xc = _ffn(xc, mt, *ffn_args, seq_len=n_ctx, mod_row=ctx_row, tm=n_ctx)
    return xl.reshape(bsz, seq, d)
```

```python
import functools

import numpy as np
import jax
import jax.numpy as jnp
from jax import lax
from jax.experimental import pallas as pl
from jax.experimental.pallas import tpu as pltpu

F32 = jnp.float32
BF16 = jnp.bfloat16
HIGHEST = lax.Precision.HIGHEST

V7X_VMEM_BYTES = 64 * 1024 * 1024
VMEM_LIMIT_BYTES = V7X_VMEM_BYTES - 8 * 1024 * 1024
LANES = 128
BF16_SUBLANES = 16

GRID_W = 64
HEAD_DIM = 64
A_Q_HEADS = 8
A_KV_HEADS = 2
A_GROUP = A_Q_HEADS // A_KV_HEADS
B_HEADS = 4
B_KEY_DIM = 128
C_HEADS = 16
NA_ROWS = 8
NA_COLS = 16
EPS = 1e-6
ROPE_THETA = 10000.0
ATTN_SCALE = HEAD_DIM ** -0.5
HGRN_SCALE = B_KEY_DIM ** -0.5
NEG_BIG = -1e30
LOG2E = 1.4426950408889634

A_Q_W = A_Q_HEADS * HEAD_DIM
A_KV_W = A_KV_HEADS * HEAD_DIM
B_W = B_HEADS * B_KEY_DIM

HGRN_CHUNK = 64
HGRN_SUB = 8
HGRN_BLOCK = 256
KV_BLOCK = 256
GQA_TQ = 512
GQA_SUB = 128
GQA_UNROLL = 4
NA_ROW_UNROLL = 8


def _cparams(*sem):
    return pltpu.CompilerParams(dimension_semantics=sem, vmem_limit_bytes=VMEM_LIMIT_BYTES)


def _const_spec(shape):
    n = len(shape)
    return pl.BlockSpec(shape, lambda *_: (0,) * n, pipeline_mode=pl.Buffered(1))


def _silu(x):
    return x / (1.0 + jnp.exp(-x))


def _rms(x, w):
    return x * lax.rsqrt(jnp.mean(x * x, axis=-1, keepdims=True) + EPS) * w


def _norm_mod(x, w, scale, shift):
    return _rms(x, w) * (1.0 + scale) + shift


def _split_bf16(x, parts):
    terms = []
    for _ in range(parts):
        hi = x.astype(BF16)
        terms.append(hi)
        x = x - hi.astype(F32)
    return terms


def _mod_body(c_ref, w_ref, b_ref, o_ref):
    s = _silu(c_ref[...])
    o_ref[0] = jnp.dot(s, w_ref[0], preferred_element_type=F32, precision=HIGHEST) + b_ref[0]


def _modulation(c, c_ctx, w_mod, b_mod):
    depth, d, n = w_mod.shape
    b = c.shape[0]
    rows = -(-(b + 1) // 8) * 8
    cc = jnp.concatenate([c, c_ctx[None, :], jnp.zeros((rows - b - 1, d), F32)], axis=0)
    tn = n // 4
    out = pl.pallas_call(
        _mod_body,
        grid=(depth, n // tn),
        in_specs=[
            pl.BlockSpec((rows, d), lambda l, j: (0, 0)),
            pl.BlockSpec((1, d, tn), lambda l, j: (l, 0, j)),
            pl.BlockSpec((1, 1, tn), lambda l, j: (l, 0, j)),
        ],
        out_specs=pl.BlockSpec((1, rows, tn), lambda l, j: (l, 0, j)),
        out_shape=jax.ShapeDtypeStruct((depth, rows, n), F32),
        compiler_params=_cparams("parallel", "parallel"),
        name="adaln_mod",
    )(cc, w_mod, b_mod.reshape(depth, 1, n))
    return out.reshape(depth, rows, 6, d)


def _mod_index(seq_len, tm, mod_row):
    if mod_row is None:
        per_seq = seq_len // tm
        return lambda i: (i // per_seq, 0, 0)
    return lambda i: (mod_row, 0, 0)


def _even_in_body(x_ref, mod_ref, nw_ref, w_ref, qw_ref, kw_ref, cos_ref, sin_ref, gmat_ref,
                  qa_ref, ka_ref, va_ref, bq_ref, z_ref, bi_ref, sg_ref, *, use_rope):
    h = _norm_mod(x_ref[...], nw_ref[...], mod_ref[0, 1:2, :], mod_ref[0, 0:1, :]).astype(BF16)
    tm = h.shape[0]
    lane = lax.broadcasted_iota(jnp.int32, (tm, LANES), 1)
    lo_half = (lane % HEAD_DIM) < (HEAD_DIM // 2)
    lo_head = lane < HEAD_DIM
    gmat = gmat_ref[...]

    def proj(a, b):
        return jnp.dot(h, w_ref[:, a:b], preferred_element_type=F32)

    def head_norm_rope(zc, w):
        ms = jnp.dot(jnp.concatenate(_split_bf16(zc * zc, 2), axis=1), gmat, preferred_element_type=F32)
        y = zc * lax.rsqrt(ms + EPS) * w
        if use_rope:
            rot = jnp.where(lo_half, pltpu.roll(y, LANES - HEAD_DIM // 2, 1), pltpu.roll(y, HEAD_DIM // 2, 1))
            y = y * cos_ref[...] + rot * sin_ref[...]
        return y

    zq = proj(0, A_Q_W)
    for c in range(A_Q_W // LANES):
        y = head_norm_rope(zq[:, c * LANES:(c + 1) * LANES], qw_ref[...]) * (ATTN_SCALE * LOG2E)
        swapped = pltpu.roll(y, HEAD_DIM, 1)
        g = (2 * c) // A_GROUP
        if g == 0:
            h0, h1 = jnp.where(lo_head, y, 0.0), jnp.where(lo_head, swapped, 0.0)
        else:
            h0, h1 = jnp.where(lo_head, 0.0, swapped), jnp.where(lo_head, 0.0, y)
        qa_ref[:, (2 * c) * LANES:(2 * c + 1) * LANES] = h0.astype(BF16)
        qa_ref[:, (2 * c + 1) * LANES:(2 * c + 2) * LANES] = h1.astype(BF16)
    o = A_Q_W
    ka_ref[...] = head_norm_rope(proj(o, o + A_KV_W), kw_ref[...]).astype(BF16)
    o += A_KV_W
    va_ref[...] = proj(o, o + A_KV_W).astype(BF16)
    o += A_KV_W
    bq_ref[...] = (_silu(proj(o, o + B_W)) * HGRN_SCALE).astype(BF16)
    o += B_W
    z_ref[...] = proj(o, o + 2 * B_W)
    o += 2 * B_W
    bi_ref[...] = proj(o, o + B_W).astype(BF16)
    o += B_W
    sg_ref[...] = _silu(proj(o, o + B_W)).astype(BF16)


def _even_in(x, modtab, norm_w, w_in, q_norm, k_norm, rope, gmat, *, seq_len, mod_row, tm):
    n, d = x.shape
    use_rope = rope is not None
    cos, sin = rope if use_rope else (jnp.zeros((8, LANES), F32),) * 2
    per_seq = seq_len // tm
    rope_spec = pl.BlockSpec((tm, LANES), lambda i: (i % per_seq, 0)) if use_rope else _const_spec((8, LANES))
    row = lambda w: pl.BlockSpec((tm, w), lambda i: (i, 0))
    widths = (2 * A_Q_W, A_KV_W, A_KV_W, B_W, 2 * B_W, B_W, B_W)
    dtypes = (BF16, BF16, BF16, BF16, F32, BF16, BF16)
    return pl.pallas_call(
        functools.partial(_even_in_body, use_rope=use_rope),
        grid=(n // tm,),
        in_specs=[
            row(d),
            pl.BlockSpec((1, 6, d), _mod_index(seq_len, tm, mod_row)),
            _const_spec((1, d)),
            _const_spec(w_in.shape),
            _const_spec((1, LANES)),
            _const_spec((1, LANES)),
            rope_spec, rope_spec,
            _const_spec((2 * LANES, LANES)),
        ],
        out_specs=[row(w) for w in widths],
        out_shape=[jax.ShapeDtypeStruct((n, w), t) for w, t in zip(widths, dtypes)],
        compiler_params=_cparams("parallel"),
        name="even_in",
    )(x, modtab, norm_w.reshape(1, d), w_in, q_norm, k_norm, cos, sin, gmat)


def _gqa_body(*refs, tq, n_lat_blk):
    if n_lat_blk:
        q_ref, kc_ref, vc_ref, kl_ref, vl_ref, o_ref, s_ref, m_ref, acc_ref = refs
    else:
        q_ref, kc_ref, vc_ref, o_ref, s_ref, m_ref, acc_ref = refs
    sq = GQA_SUB
    lo_head = lax.broadcasted_iota(jnp.int32, (sq, LANES), 1) < HEAD_DIM
    lo_head_kv = lax.broadcasted_iota(jnp.int32, (KV_BLOCK, LANES), 1) < HEAD_DIM
    units = [(g, sub) for g in range(A_KV_HEADS) for sub in range(tq // sq)]

    def lat_rows(j):
        return pl.ds(pl.multiple_of(j * KV_BLOCK, KV_BLOCK), KV_BLOCK)

    def load_q(g, sub):
        return jnp.concatenate(
            [q_ref[0, sub * sq:(sub + 1) * sq, (A_GROUP * g + j) * LANES:(A_GROUP * g + j + 1) * LANES]
             for j in range(A_GROUP)], axis=0)

    def scores(slot, q, j, k):
        s = lax.dot_general(q, k, (((1,), (1,)), ((), ())), preferred_element_type=F32)
        s_ref[slot, j] = s
        blk_max = s[:, 0:LANES]
        for c in range(1, KV_BLOCK // LANES):
            blk_max = jnp.maximum(blk_max, s[:, c * LANES:(c + 1) * LANES])
        return blk_max

    def probs(slot, j):
        m = m_ref[slot]
        return jnp.exp2(s_ref[slot, j] - jnp.concatenate([m] * (KV_BLOCK // LANES), axis=1)).astype(BF16)

    def weighted(g, p, v):
        own = lo_head_kv if g == 0 else ~lo_head_kv
        own = jnp.concatenate([own] * (v.shape[0] // KV_BLOCK), axis=0)
        return jnp.dot(p, jnp.where(own, v, jnp.ones_like(v)), preferred_element_type=F32)

    def finish(g, sub):
        acc = acc_ref[...]
        o = acc / pltpu.roll(acc, HEAD_DIM, 1)
        for c in range(A_GROUP // 2):
            o0 = o[(2 * c) * sq:(2 * c + 1) * sq]
            o1 = o[(2 * c + 1) * sq:(2 * c + 2) * sq]
            if g == 0:
                pair = jnp.where(lo_head, o0, pltpu.roll(o1, HEAD_DIM, 1))
            else:
                pair = jnp.where(lo_head, pltpu.roll(o0, HEAD_DIM, 1), o1)
            col = (A_GROUP // 2 * g + c) * LANES
            o_ref[0, sub * sq:(sub + 1) * sq, col:col + LANES] = pair.astype(BF16)

    def phase(idx):
        score = units[idx] if idx < len(units) else None
        pv = units[idx - 1] if idx >= 1 else None
        slot_s, slot_p = idx % 2, (idx - 1) % 2
        if score:
            m_ref[slot_s] = scores(slot_s, load_q(*score), 0, kc_ref[0])
        if pv:
            acc_ref[...] = weighted(pv[0], probs(slot_p, 0), vc_ref[0])
        if n_lat_blk:
            def step(i, carry):
                blk_max, ps = None, []
                for u in range(GQA_UNROLL):
                    j = i * GQA_UNROLL + u
                    if score:
                        cur = scores(slot_s, load_q(*score), j + 1, kl_ref[0, lat_rows(j), :])
                        blk_max = cur if blk_max is None else jnp.maximum(blk_max, cur)
                    if pv:
                        ps.append(probs(slot_p, j + 1))
                if score:
                    m_ref[slot_s] = jnp.maximum(m_ref[slot_s], blk_max)
                if pv:
                    span = GQA_UNROLL * KV_BLOCK
                    v = vl_ref[0, pl.ds(pl.multiple_of(i * span, span), span), :]
                    acc_ref[...] += weighted(pv[0], jnp.concatenate(ps, axis=1), v)
                return carry
            lax.fori_loop(0, n_lat_blk // GQA_UNROLL, step, 0)
        if score:
            m_ref[slot_s] = jnp.broadcast_to(jnp.max(m_ref[slot_s], axis=1, keepdims=True), m_ref.shape[1:])
        if pv:
            finish(*pv)

    for idx in range(len(units) + 1):
        phase(idx)


def _gqa(q, k_ctx, v_ctx, k_lat=None, v_lat=None, *, tq):
    b, nq, _ = q.shape
    n_ctx = k_ctx.shape[1]
    assert n_ctx == KV_BLOCK
    with_lat = k_lat is not None
    n_lat = k_lat.shape[1] if with_lat else 0
    kv_spec = lambda n: pl.BlockSpec((1, n, A_KV_W), lambda i, j: (i, 0, 0))
    in_specs = [pl.BlockSpec((1, tq, 2 * A_Q_W), lambda i, j: (i, j, 0)), kv_spec(n_ctx), kv_spec(n_ctx)]
    args = [q, k_ctx, v_ctx]
    if with_lat:
        in_specs += [kv_spec(n_lat), kv_spec(n_lat)]
        args += [k_lat, v_lat]
    rows = A_GROUP * GQA_SUB
    n_blk = 1 + n_lat // KV_BLOCK
    assert (n_blk - 1) % GQA_UNROLL == 0 and tq % GQA_SUB == 0
    return pl.pallas_call(
        functools.partial(_gqa_body, tq=tq, n_lat_blk=n_lat // KV_BLOCK),
        grid=(b, nq // tq),
        in_specs=in_specs,
        out_specs=pl.BlockSpec((1, tq, A_Q_W), lambda i, j: (i, j, 0)),
        out_shape=jax.ShapeDtypeStruct((b, nq, A_Q_W), BF16),
        scratch_shapes=[pltpu.VMEM((2, n_blk, rows, KV_BLOCK), F32), pltpu.VMEM((2, rows, LANES), F32),
                        pltpu.VMEM((rows, LANES), F32)],
        compiler_params=_cparams("parallel", "parallel"),
        name="gqa_lat" if with_lat else "gqa_ctx",
    )(*args)


def _hgrn_masks(rev):
    c = HGRN_CHUNK
    t = lax.broadcasted_iota(jnp.int32, (c, c), 0)
    s = lax.broadcasted_iota(jnp.int32, (c, c), 1)
    causal = (s >= t) if rev else (s <= t)
    diag = ((t // HGRN_SUB) == (s // HGRN_SUB)) & causal
    levels = []
    size = 2 * HGRN_SUB
    while size <= c:
        half = size // 2
        t_hi, s_hi = (t % size) >= half, (s % size) >= half
        pair = (~t_hi & s_hi) if rev else (t_hi & ~s_hi)
        levels.append((size, ((t // size) == (s // size)) & pair))
        size *= 2
    return diag, levels


def _hgrn_chunk(row0, rev, q_ref, z_ref, v_ref, lb_ref, tri_ref, sel_ref, o_ref, st_ref, cbuf, bbuf, pcat):
    c, sub = HGRN_CHUNK, HGRN_SUB
    rows = pl.ds(row0, c)
    diag_mask, level_masks = _hgrn_masks(rev)
    z = z_ref[0, rows, :]
    log_lb, log_1m_lb = lb_ref[0, 0:1, :], lb_ref[0, 1:2, :]
    log_sig = jnp.minimum(z, 0.0) - jnp.log(1.0 + jnp.exp(-jnp.abs(z)))
    cterm = log_1m_lb + log_sig
    log_f = jnp.maximum(log_lb, cterm) + jnp.log(1.0 + jnp.exp(-jnp.abs(log_lb - cterm)))
    lk_all = (cterm - z) * LOG2E
    b_all = jnp.dot(tri_ref[...], jnp.concatenate(_split_bf16(log_f * LOG2E, 3), axis=0),
                    preferred_element_type=F32)
    per_head = []
    for hd in range(B_HEADS):
        cols = slice(hd * LANES, (hd + 1) * LANES)
        q = q_ref[0, rows, cols].astype(F32)
        lk, b = lk_all[:, cols], b_all[:, cols]
        cbuf[hd] = b - lk
        bbuf[hd] = b
        for u in range(sub):
            cu = jnp.concatenate(
                [jnp.broadcast_to(cbuf[hd, m * sub + u:m * sub + u + 1, :], (sub, LANES)) for m in range(c // sub)], 0)
            pcat[hd * c:(hd + 1) * c, u * LANES:(u + 1) * LANES] = (
                q * jnp.exp2(jnp.minimum(b - cu, 0.0))).astype(BF16)
        per_head.append((q, lk, b))
    a_diag = jnp.dot(pcat[...], sel_ref[...], preferred_element_type=F32)
    for hd in range(B_HEADS):
        cols = slice(hd * LANES, (hd + 1) * LANES)
        q, lk, b = per_head[hd]
        v = v_ref[0, rows, cols]
        a = jnp.where(diag_mask, a_diag[hd * c:(hd + 1) * c], 0.0)
        for size, mask in level_masks:
            half = size // 2
            ref_rows = [m * size + (half if rev else half - 1) for m in range(c // size)]
            beta = jnp.concatenate(
                [jnp.broadcast_to(bbuf[hd, r:r + 1, :], (size, LANES)) for r in ref_rows], 0)
            e = -jnp.abs(b - beta)
            lvl = lax.dot_general((q * jnp.exp2(e)).astype(BF16), jnp.exp2(e + lk).astype(BF16),
                                  (((1,), (1,)), ((), ())), preferred_element_type=F32)
            a = a + jnp.where(mask, lvl, 0.0)
        b_end = bbuf[hd, 0:1, :] if rev else bbuf[hd, c - 1:c, :]
        st = st_ref[0, 0, hd]
        o = jnp.dot(a.astype(BF16), v, preferred_element_type=F32)
        o = o + lax.dot_general((q * jnp.exp2(b)).astype(BF16), st.astype(BF16), (((1,), (1,)), ((), ())),
                                preferred_element_type=F32)
        o_ref[0, 0, rows, cols] = o
        kd = jnp.exp2(lk + (b_end - b)).astype(BF16)
        upd = lax.dot_general(v, kd, (((0,), (0,)), ((), ())), preferred_element_type=F32)
        st_ref[0, 0, hd] = st * jnp.exp2(b_end) + upd


def _hgrn_body(q_ref, z_ref, v_ref, lb_ref, tri_ref, sel_ref, s0_ref, o_ref, st_ref, kbuf, bbuf, pcat):
    d, j = pl.program_id(1), pl.program_id(2)
    n_chunks = HGRN_BLOCK // HGRN_CHUNK

    @pl.when(j == 0)
    def _():
        st_ref[...] = s0_ref[...]

    for rev in (False, True):
        @pl.when(d == int(rev))
        def _(rev=rev):
            for i in range(n_chunks):
                ci = (n_chunks - 1 - i) if rev else i
                _hgrn_chunk(ci * HGRN_CHUNK, rev, q_ref, z_ref, v_ref, lb_ref, tri_ref, sel_ref, o_ref, st_ref,
                            kbuf.at[ci], bbuf.at[ci], pcat.at[ci])


def _hgrn(q, z, v, lbtab, tri, sel, s0):
    b, n, _ = q.shape
    nblk = n // HGRN_BLOCK
    c = HGRN_CHUNK

    def blk(d, j):
        return jnp.where(d == 0, j, nblk - 1 - j)

    tok = lambda i, d, j: (i, blk(d, j), 0)
    st_spec = pl.BlockSpec((1, 1, B_HEADS, LANES, LANES), lambda i, d, j: (i, d, 0, 0, 0))
    return pl.pallas_call(
        _hgrn_body,
        grid=(b, 2, nblk),
        in_specs=[
            pl.BlockSpec((1, HGRN_BLOCK, B_W), tok),
            pl.BlockSpec((1, HGRN_BLOCK, B_W), lambda i, d, j: (i, blk(d, j), d)),
            pl.BlockSpec((1, HGRN_BLOCK, B_W), tok),
            pl.BlockSpec((1, 8, B_W), lambda i, d, j: (d, 0, 0)),
            pl.BlockSpec((None, c, 3 * c), lambda i, d, j: (d, 0, 0)),
            _const_spec(sel.shape),
            st_spec,
        ],
        out_specs=[
            pl.BlockSpec((1, 1, HGRN_BLOCK, B_W), lambda i, d, j: (d, i, blk(d, j), 0)),
            st_spec,
        ],
        out_shape=[jax.ShapeDtypeStruct((2, b, n, B_W), F32),
                   jax.ShapeDtypeStruct(s0.shape, F32)],
        scratch_shapes=[pltpu.VMEM((HGRN_BLOCK // c, B_HEADS, c, LANES), F32),
                        pltpu.VMEM((HGRN_BLOCK // c, B_HEADS, c, LANES), F32),
                        pltpu.VMEM((HGRN_BLOCK // c, B_HEADS * c, HGRN_SUB * LANES), BF16)],
        compiler_params=_cparams("parallel", "parallel", "arbitrary"),
        name="hgrn_scan",
    )(q, z, v, lbtab, tri, sel, s0)


def _odd_in_body(x_ref, mod_ref, nw_ref, w_ref, q_ref, k_ref, v_ref):
    h = _norm_mod(x_ref[...], nw_ref[...], mod_ref[0, 1:2, :], mod_ref[0, 0:1, :]).astype(BF16)
    cw = q_ref.shape[1]
    q_ref[...] = (jnp.dot(h, w_ref[:, 0:cw], preferred_element_type=F32) * ATTN_SCALE).astype(BF16)
    k_ref[...] = jnp.dot(h, w_ref[:, cw:2 * cw], preferred_element_type=F32).astype(BF16)
    v_ref[...] = jnp.dot(h, w_ref[:, 2 * cw:3 * cw], preferred_element_type=F32).astype(BF16)


def _odd_in(x, modtab, norm_w, w_qkv, *, seq_len, mod_row, tm):
    n, d = x.shape
    cw = w_qkv.shape[1] // 3
    row = pl.BlockSpec((tm, cw), lambda i: (i, 0))
    return pl.pallas_call(
        _odd_in_body,
        grid=(n // tm,),
        in_specs=[
            pl.BlockSpec((tm, d), lambda i: (i, 0)),
            pl.BlockSpec((1, 6, d), _mod_index(seq_len, tm, mod_row)),
            _const_spec((1, d)),
            _const_spec(w_qkv.shape),
        ],
        out_specs=[row, row, row],
        out_shape=[jax.ShapeDtypeStruct((n, cw), BF16)] * 3,
        compiler_params=_cparams("parallel"),
        name="odd_in",
    )(x, modtab, norm_w.reshape(1, d), w_qkv)


def _na_body(ql_ref, kl_ref, vl_ref, qc_ref, kc_ref, vc_ref, bias_ref, ol_ref, oc_ref, *, rows):
    band = NA_ROWS * GRID_W
    lane_q = lax.broadcasted_iota(jnp.int32, (GRID_W, LANES), 1) < HEAD_DIM
    kc, vc = kc_ref[0], vc_ref[0]
    nt = (((1,), (1,)), ((), ()))

    def two_heads(q, lo_mask):
        zero = jnp.zeros_like(q)
        return jnp.concatenate([jnp.where(lo_mask, q, zero), jnp.where(lo_mask, zero, q)], axis=0)

    def merge(o, n, lo_mask):
        return jnp.where(lo_mask, o[:n], o[n:])

    def with_ones(v):
        return jnp.concatenate([v, jnp.ones_like(v)], axis=1)

    vc_aug = with_ones(vc)

    def row_group(i, carry):
        n2 = 2 * GRID_W
        rs = [i * NA_ROW_UNROLL + u for u in range(NA_ROW_UNROLL)]
        q = jnp.concatenate(
            [two_heads(ql_ref[0, pl.ds(pl.multiple_of(r * GRID_W, GRID_W), GRID_W), :], lane_q) for r in rs], axis=0)
        s_ctx = lax.dot_general(q, kc, nt, preferred_element_type=F32)
        p_ctx, p_loc, v_band = [], [], []
        for u, r in enumerate(rs):
            r0 = jnp.clip(r - NA_ROWS // 2, 0, rows - NA_ROWS)
            kstart = pl.multiple_of(r0 * GRID_W, GRID_W)
            s_loc = lax.dot_general(q[u * n2:(u + 1) * n2], kl_ref[0, pl.ds(kstart, band), :], nt,
                                    preferred_element_type=F32) + bias_ref[0, r - r0]
            s_c = s_ctx[u * n2:(u + 1) * n2]
            m = jnp.maximum(jnp.max(s_loc, axis=1, keepdims=True), jnp.max(s_c, axis=1, keepdims=True))
            p_loc.append(jnp.exp(s_loc - m).astype(BF16))
            p_ctx.append(jnp.exp(s_c - m).astype(BF16))
            v_band.append(vl_ref[0, pl.ds(kstart, band), :])
        o_ctx = jnp.dot(jnp.concatenate(p_ctx, axis=0), vc_aug, preferred_element_type=F32)
        for u, r in enumerate(rs):
            o = jnp.dot(p_loc[u], with_ones(v_band[u]), preferred_element_type=F32) + o_ctx[u * n2:(u + 1) * n2]
            o = o[:, :LANES] / o[:, LANES:]
            ol_ref[0, pl.ds(pl.multiple_of(r * GRID_W, GRID_W), GRID_W), :] = (
                merge(o, GRID_W, lane_q).astype(BF16))
        return carry

    lax.fori_loop(0, rows // NA_ROW_UNROLL, row_group, 0)

    n_ctx = qc_ref.shape[1]
    lane_c = lax.broadcasted_iota(jnp.int32, (n_ctx, LANES), 1) < HEAD_DIM
    q = two_heads(qc_ref[0], lane_c)
    s = lax.dot_general(q, kc, nt, preferred_element_type=F32)
    p = jnp.exp(s - jnp.max(s, axis=1, keepdims=True)).astype(BF16)
    o = jnp.dot(p, vc_aug, preferred_element_type=F32)
    oc_ref[0] = merge(o[:, :LANES] / o[:, LANES:], n_ctx, lane_c).astype(BF16)


def _na_bias(rpb):
    heads, n_dy, n_dx = rpb.shape
    qc = np.arange(GRID_W)
    kc = np.arange(GRID_W)
    c_start = np.clip(qc - NA_COLS // 2, 0, GRID_W - NA_COLS)
    in_win = (kc[None, :] >= c_start[:, None]) & (kc[None, :] < c_start[:, None] + NA_COLS)
    dx = kc[None, :] - qc[:, None] + (NA_COLS - 1)
    onehot = ((dx[:, :, None] == np.arange(n_dx)[None, None, :]) & in_win[:, :, None]).astype(np.float32)
    t = jnp.einsum("hyx,qcx->hqyc", rpb.astype(F32), jnp.asarray(onehot), precision=HIGHEST)
    t = jnp.where(jnp.asarray(in_win)[None, :, None, :], t, NEG_BIG).reshape(heads, GRID_W, n_dy * GRID_W)
    width = NA_ROWS * GRID_W
    tab = jnp.stack([t[:, :, (NA_ROWS - 1 - var) * GRID_W:(NA_ROWS - 1 - var) * GRID_W + width]
                     for var in range(NA_ROWS)], axis=1)
    tab = tab.reshape(heads // 2, 2, NA_ROWS, GRID_W, width)
    return tab.transpose(0, 2, 1, 3, 4).reshape(heads // 2, NA_ROWS, 2 * GRID_W, width)


def _na(q_lat, k_lat, v_lat, q_ctx, k_ctx, v_ctx, bias):
    b, n, w = q_lat.shape
    n_ctx = q_ctx.shape[1]
    rows = n // GRID_W
    assert rows >= NA_ROWS and rows % NA_ROW_UNROLL == 0
    pairs = w // LANES
    lat = pl.BlockSpec((1, n, LANES), lambda p, i: (i, 0, p))
    ctx = pl.BlockSpec((1, n_ctx, LANES), lambda p, i: (i, 0, p))
    return pl.pallas_call(
        functools.partial(_na_body, rows=rows),
        grid=(pairs, b),
        in_specs=[lat, lat, lat, ctx, ctx, ctx,
                  pl.BlockSpec((1,) + bias.shape[1:], lambda p, i: (p, 0, 0, 0))],
        out_specs=[lat, ctx],
        out_shape=[jax.ShapeDtypeStruct((b, n, w), BF16), jax.ShapeDtypeStruct((b, n_ctx, w), BF16)],
        compiler_params=_cparams("parallel", "parallel"),
        name="na_attn",
    )(q_lat, k_lat, v_lat, q_ctx, k_ctx, v_ctx, bias)


def _post(acc, post_w, gate, x):
    return x + gate * _rms(acc, post_w)


def _even_out_body(x_ref, mod_ref, oa_ref, ob_ref, sg_ref, onw_ref, w_ref, pw_ref, y_ref):
    acc = jnp.dot(oa_ref[...], w_ref[0:A_Q_W, :], preferred_element_type=F32)
    ob = ob_ref[0] + ob_ref[1]
    parts = []
    for hd in range(B_HEADS):
        cols = slice(hd * LANES, (hd + 1) * LANES)
        parts.append((_rms(ob[:, cols], onw_ref[...]) * sg_ref[:, cols].astype(F32)).astype(BF16))
    acc = acc + jnp.dot(jnp.concatenate(parts, axis=1), w_ref[A_Q_W:, :], preferred_element_type=F32)
    y_ref[...] = _post(acc, pw_ref[...], mod_ref[0, 2:3, :], x_ref[...])


def _even_out(x, modtab, oa, ob, sg, out_norm, w_out, post_w, *, seq_len, mod_row, tm):
    n, d = x.shape
    row = lambda w: pl.BlockSpec((tm, w), lambda i: (i, 0))
    return pl.pallas_call(
        _even_out_body,
        grid=(n // tm,),
        in_specs=[
            row(d),
            pl.BlockSpec((1, 6, d), _mod_index(seq_len, tm, mod_row)),
            row(A_Q_W),
            pl.BlockSpec((2, tm, B_W), lambda i: (0, i, 0)),
            row(B_W),
            _const_spec((1, LANES)),
            _const_spec(w_out.shape),
            _const_spec((1, d)),
        ],
        out_specs=row(d),
        out_shape=jax.ShapeDtypeStruct((n, d), F32),
        compiler_params=_cparams("parallel"),
        name="even_out",
    )(x, modtab, oa, ob, sg, out_norm.reshape(1, LANES), w_out, post_w.reshape(1, d))


def _odd_out_body(x_ref, mod_ref, o_ref, w_ref, pw_ref, y_ref):
    acc = jnp.dot(o_ref[...], w_ref[...], preferred_element_type=F32)
    y_ref[...] = _post(acc, pw_ref[...], mod_ref[0, 2:3, :], x_ref[...])


def _odd_out(x, modtab, o, w_out, post_w, *, seq_len, mod_row, tm):
    n, d = x.shape
    row = lambda w: pl.BlockSpec((tm, w), lambda i: (i, 0))
    return pl.pallas_call(
        _odd_out_body,
        grid=(n // tm,),
        in_specs=[
            row(d),
            pl.BlockSpec((1, 6, d), _mod_index(seq_len, tm, mod_row)),
            row(o.shape[1]),
            _const_spec(w_out.shape),
            _const_spec((1, d)),
        ],
        out_specs=row(d),
        out_shape=jax.ShapeDtypeStruct((n, d), F32),
        compiler_params=_cparams("parallel"),
        name="odd_out",
    )(x, modtab, o, w_out, post_w.reshape(1, d))


FFN_HALO = BF16_SUBLANES
FFN_TN = 256
FFN_TM = 1024


def _ffn_body(x_ref, xp_ref, xn_ref, mod_ref, nw_ref, wup_ref, cw_ref, cb_ref, wdn_ref, pw_ref, y_ref,
              hbuf, ua, ug, act, *, seq_len):
    tm, d = x_ref.shape
    dff = wdn_ref.shape[0]
    halo = FFN_HALO
    scale, shift = mod_ref[0, 4:5, :], mod_ref[0, 3:4, :]
    x = x_ref[...]
    row0 = pl.program_id(0) * tm
    top_in = (row0 % seq_len) != 0
    bottom_in = ((row0 + tm) % seq_len) != 0
    hbuf[0:halo, :] = jnp.where(top_in, _norm_mod(xp_ref[...], nw_ref[...], scale, shift), 0.0).astype(BF16)
    hbuf[halo:halo + tm, :] = _norm_mod(x, nw_ref[...], scale, shift).astype(BF16)
    hbuf[halo + tm:, :] = jnp.where(bottom_in, _norm_mod(xn_ref[...], nw_ref[...], scale, shift), 0.0).astype(BF16)
    h = hbuf[...]
    for j in range(dff // FFN_TN):
        halves = []
        for buf, base in ((ua, 0), (ug, dff)):
            cols = slice(base + j * FFN_TN, base + (j + 1) * FFN_TN)
            buf[...] = jnp.dot(h, wup_ref[:, cols], preferred_element_type=F32)
            halves.append(
                cw_ref[0:1, cols] * buf[halo - 1:halo - 1 + tm, :]
                + cw_ref[1:2, cols] * buf[halo:halo + tm, :]
                + cw_ref[2:3, cols] * buf[halo + 1:halo + 1 + tm, :]
                + cb_ref[0:1, cols])
        a, g = halves
        act[:, j * FFN_TN:(j + 1) * FFN_TN] = (_silu(g) * a).astype(BF16)
    out = jnp.dot(act[...], wdn_ref[...], preferred_element_type=F32)
    y_ref[...] = _post(out, pw_ref[...], mod_ref[0, 5:6, :], x)


def _ffn(x, modtab, norm_w, w_up, conv_w, conv_b, w_down, post_w, *, seq_len, mod_row, tm):
    n, d = x.shape
    dff = w_down.shape[0]
    assert dff % FFN_TN == 0 and tm % FFN_HALO == 0 and seq_len % tm == 0
    per_tile = tm // FFN_HALO
    last = n // FFN_HALO - 1
    return pl.pallas_call(
        functools.partial(_ffn_body, seq_len=seq_len),
        grid=(n // tm,),
        in_specs=[
            pl.BlockSpec((tm, d), lambda i: (i, 0)),
            pl.BlockSpec((FFN_HALO, d), lambda i: (jnp.maximum(i * per_tile - 1, 0), 0)),
            pl.BlockSpec((FFN_HALO, d), lambda i: (jnp.minimum((i + 1) * per_tile, last), 0)),
            pl.BlockSpec((1, 6, d), _mod_index(seq_len, tm, mod_row)),
            _const_spec((1, d)),
            _const_spec(w_up.shape),
            _const_spec(conv_w.shape),
            _const_spec((1, 2 * dff)),
            _const_spec(w_down.shape),
            _const_spec((1, d)),
        ],
        out_specs=pl.BlockSpec((tm, d), lambda i: (i, 0)),
        out_shape=jax.ShapeDtypeStruct((n, d), F32),
        scratch_shapes=[pltpu.VMEM((tm + 2 * FFN_HALO, d), BF16),
                        pltpu.VMEM((tm + 2 * FFN_HALO, FFN_TN), F32),
                        pltpu.VMEM((tm + 2 * FFN_HALO, FFN_TN), F32),
                        pltpu.VMEM((tm, dff), BF16)],
        compiler_params=_cparams("parallel"),
        name="conv_ffn",
    )(x, x, x, modtab, norm_w.reshape(1, d), w_up, conv_w, conv_b.reshape(1, 2 * dff), w_down,
      post_w.reshape(1, d))


def _rope_tables(n_tokens):
    t = jnp.arange(n_tokens)
    row = (t // GRID_W).astype(F32)
    col = (t % GRID_W).astype(F32)
    n_freq = HEAD_DIM // 4
    inv_freq = ROPE_THETA ** (-jnp.arange(n_freq, dtype=F32) / n_freq)
    ang = jnp.concatenate([row[:, None] * inv_freq, col[:, None] * inv_freq], axis=-1)
    cos, sin = jnp.cos(ang), jnp.sin(ang)
    cos_t = jnp.tile(cos, (1, LANES // (HEAD_DIM // 2)))
    sin_t = jnp.tile(jnp.concatenate([-sin, sin], axis=-1), (1, LANES // HEAD_DIM))
    return cos_t, sin_t


def _hgrn_tables():
    c = HGRN_CHUNK
    lower = jnp.tril(jnp.ones((c, c), F32))
    tri = jnp.stack([lower, lower.T])
    tri = jnp.concatenate([tri] * 3, axis=2).astype(BF16)
    u = jnp.repeat(jnp.arange(HGRN_SUB), LANES)
    sel = (u[:, None] == (jnp.arange(c) % HGRN_SUB)[None, :]).astype(BF16)
    return tri, sel


def _lb_table(lb):
    pad = jnp.zeros((2, 6, lb.shape[-1]), F32)
    return jnp.concatenate([jnp.log(lb)[:, None], jnp.log1p(-lb)[:, None], pad], axis=1)


def kernel(x, c, ctx, c_ctx, w_mod, b_mod, norm_pre_mix, norm_post_mix, norm_pre_ffn, norm_post_ffn, even_w_in, even_w_out, even_q_norm, even_k_norm, hgrn_lb_logits, hgrn_out_norm, odd_w_qkv, odd_w_out, odd_rpb, ffn_w_up, ffn_conv_w, ffn_conv_b, ffn_w_down):
    bsz, seq, d = x.shape
    n_ctx = ctx.shape[1]
    depth = w_mod.shape[0]
    n_even = even_w_in.shape[0]

    modtab = _modulation(c, c_ctx, w_mod, b_mod)
    ctx_row = bsz
    rope = _rope_tables(seq)
    gmat = jnp.kron(jnp.eye(LANES // HEAD_DIM, dtype=F32), jnp.full((HEAD_DIM, HEAD_DIM), 1.0 / HEAD_DIM, F32))
    gmat = jnp.concatenate([gmat, gmat], axis=0).astype(BF16)
    tri, sel = _hgrn_tables()
    p_lb = jax.nn.softmax(hgrn_lb_logits.astype(F32), axis=0)
    lb_all = jnp.concatenate([jnp.zeros_like(p_lb[:1]), jnp.cumsum(p_lb[1:], axis=0)], axis=0)

    xl = x.reshape(bsz * seq, d)
    xc = ctx.reshape(bsz * n_ctx, d)
    tm_lat = min(512, seq)
    tm_ctx = min(512, bsz * n_ctx)
    lat_kw = dict(seq_len=seq, mod_row=None, tm=tm_lat)
    ctx_kw = dict(seq_len=n_ctx, mod_row=ctx_row, tm=tm_ctx)

    for l in range(depth):
        with_ctx_out = l < depth - 1
        mt = modtab[l]
        if l % 2 == 0:
            e = l // 2
            w_in = even_w_in[e].astype(BF16)
            w_out = even_w_out[e].astype(BF16)
            qn = jnp.tile(even_q_norm[e], LANES // HEAD_DIM).reshape(1, LANES)
            kn = jnp.tile(even_k_norm[e], LANES // HEAD_DIM).reshape(1, LANES)
            pl_ = _even_in(xl, mt, norm_pre_mix[l], w_in, qn, kn, rope, gmat, **lat_kw)
            pc_ = _even_in(xc, mt, norm_pre_mix[l], w_in, qn, kn, None, gmat, **ctx_kw)
            qa_l, ka_l, va_l, bq_l, z_l, bi_l, sg_l = pl_
            qa_c, ka_c, va_c, bq_c, z_c, bi_c, sg_c = pc_
            r3 = lambda a, n: a.reshape(bsz, n, a.shape[-1])
            oa_l = _gqa(r3(qa_l, seq), r3(ka_c, n_ctx), r3(va_c, n_ctx), r3(ka_l, seq), r3(va_l, seq),
                        tq=min(GQA_TQ, seq))
            lbtab = _lb_table(lb_all[e])
            s0 = jnp.zeros((bsz, 2, B_HEADS, LANES, LANES), F32)
            ob_c, s_ctx = _hgrn(r3(bq_c, n_ctx), r3(z_c, n_ctx), r3(bi_c, n_ctx), lbtab, tri, sel, s0)
            ob_l, _ = _hgrn(r3(bq_l, seq), r3(z_l, seq), r3(bi_l, seq), lbtab, tri, sel, s_ctx)
            xl_new = _even_out(xl, mt, oa_l.reshape(bsz * seq, A_Q_W), ob_l.reshape(2, bsz * seq, B_W), sg_l,
                               hgrn_out_norm[e], w_out, norm_post_mix[l], **lat_kw)
            if with_ctx_out:
                oa_c = _gqa(r3(qa_c, n_ctx), r3(ka_c, n_ctx), r3(va_c, n_ctx), tq=n_ctx)
                xc = _even_out(xc, mt, oa_c.reshape(bsz * n_ctx, A_Q_W), ob_c.reshape(2, bsz * n_ctx, B_W), sg_c,
                               hgrn_out_norm[e], w_out, norm_post_mix[l], **ctx_kw)
            xl = xl_new
        else:
            o = l // 2
            w_qkv = odd_w_qkv[o].astype(BF16)
            w_out = odd_w_out[o].astype(BF16)
            ql, kl, vl = _odd_in(xl, mt, norm_pre_mix[l], w_qkv, **lat_kw)
            qc, kc, vc = _odd_in(xc, mt, norm_pre_mix[l], w_qkv, **ctx_kw)
            r3 = lambda a, n: a.reshape(bsz, n, a.shape[-1])
            o_l, o_c = _na(r3(ql, seq), r3(kl, seq), r3(vl, seq), r3(qc, n_ctx), r3(kc, n_ctx), r3(vc, n_ctx),
                           _na_bias(odd_rpb[o]))
            xl = _odd_out(xl, mt, o_l.reshape(bsz * seq, -1), w_out, norm_post_mix[l], **lat_kw)
            if with_ctx_out:
                xc = _odd_out(xc, mt, o_c.reshape(bsz * n_ctx, -1), w_out, norm_post_mix[l], **ctx_kw)

        w_up = ffn_w_up[l].astype(BF16)
        w_dn = ffn_w_down[l].astype(BF16)
        ffn_args = (norm_pre_ffn[l], w_up, ffn_conv_w[l], ffn_conv_b[l], w_dn, norm_post_ffn[l])
        xl = _ffn(xl, mt, *ffn_args, seq_len=seq, mod_row=None, tm=min(FFN_TM, seq))
        if with_ctx_out:
            xc = _ffn(xc, mt, *ffn_args, seq_len=n_ctx, mod_row=ctx_row, tm=n_ctx)
    return xl.reshape(bsz, seq, d)
```

```python
import functools

import numpy as np
import jax
import jax.numpy as jnp
from jax import lax
from jax.experimental import pallas as pl
from jax.experimental.pallas import tpu as pltpu

F32 = jnp.float32
BF16 = jnp.bfloat16
HIGHEST = lax.Precision.HIGHEST

V7X_VMEM_BYTES = 64 * 1024 * 1024
VMEM_LIMIT_BYTES = V7X_VMEM_BYTES - 8 * 1024 * 1024
LANES = 128
BF16_SUBLANES = 16

GRID_W = 64
HEAD_DIM = 64
A_Q_HEADS = 8
A_KV_HEADS = 2
A_GROUP = A_Q_HEADS // A_KV_HEADS
B_HEADS = 4
B_KEY_DIM = 128
C_HEADS = 16
NA_ROWS = 8
NA_COLS = 16
EPS = 1e-6
ROPE_THETA = 10000.0
ATTN_SCALE = HEAD_DIM ** -0.5
HGRN_SCALE = B_KEY_DIM ** -0.5
NEG_BIG = -1e30
LOG2E = 1.4426950408889634

A_Q_W = A_Q_HEADS * HEAD_DIM
A_KV_W = A_KV_HEADS * HEAD_DIM
B_W = B_HEADS * B_KEY_DIM

HGRN_CHUNK = 64
HGRN_SUB = 8
HGRN_BLOCK = 256
KV_BLOCK = 256
GQA_TQ = 512
GQA_SUB = 128
GQA_UNROLL = 4
NA_ROW_UNROLL = 16


def _cparams(*sem):
    return pltpu.CompilerParams(dimension_semantics=sem, vmem_limit_bytes=VMEM_LIMIT_BYTES)


def _const_spec(shape):
    n = len(shape)
    return pl.BlockSpec(shape, lambda *_: (0,) * n, pipeline_mode=pl.Buffered(1))


def _layer_spec(stack, layer):
    return pl.BlockSpec((None,) + stack.shape[1:], lambda *_: (layer, 0, 0), pipeline_mode=pl.Buffered(1))


def _silu(x):
    return x / (1.0 + jnp.exp(-x))


def _rms(x, w):
    return x * lax.rsqrt(jnp.mean(x * x, axis=-1, keepdims=True) + EPS) * w


def _norm_mod(x, w, scale, shift):
    return _rms(x, w) * (1.0 + scale) + shift


def _split_bf16(x, parts):
    terms = []
    for _ in range(parts):
        hi = x.astype(BF16)
        terms.append(hi)
        x = x - hi.astype(F32)
    return terms


CAST_ROWS = 512


def _cast_body(x_ref, o_ref):
    o_ref[...] = x_ref[...].astype(BF16)


def _to_bf16(w):
    w2 = w.reshape(-1, w.shape[-1])
    rows, cols = w2.shape
    tr = min(CAST_ROWS, rows)
    assert rows % tr == 0
    spec = pl.BlockSpec((tr, cols), lambda i: (i, 0))
    out = pl.pallas_call(
        _cast_body,
        grid=(rows // tr,),
        in_specs=[spec],
        out_specs=spec,
        out_shape=jax.ShapeDtypeStruct((rows, cols), BF16),
        compiler_params=_cparams("parallel"),
        name="cast_bf16",
    )(w2)
    return out.reshape(w.shape)


def _mod_body(c_ref, w_ref, b_ref, o_ref):
    s = _silu(c_ref[...])
    o_ref[0] = jnp.dot(s, w_ref[0], preferred_element_type=F32, precision=HIGHEST) + b_ref[0]


def _modulation(c, c_ctx, w_mod, b_mod):
    depth, d, n = w_mod.shape
    b = c.shape[0]
    rows = -(-(b + 1) // 8) * 8
    cc = jnp.concatenate([c, c_ctx[None, :], jnp.zeros((rows - b - 1, d), F32)], axis=0)
    tn = n // 4
    out = pl.pallas_call(
        _mod_body,
        grid=(depth, n // tn),
        in_specs=[
            pl.BlockSpec((rows, d), lambda l, j: (0, 0)),
            pl.BlockSpec((1, d, tn), lambda l, j: (l, 0, j)),
            pl.BlockSpec((1, 1, tn), lambda l, j: (l, 0, j)),
        ],
        out_specs=pl.BlockSpec((1, rows, tn), lambda l, j: (l, 0, j)),
        out_shape=jax.ShapeDtypeStruct((depth, rows, n), F32),
        compiler_params=_cparams("parallel", "parallel"),
        name="adaln_mod",
    )(cc, w_mod, b_mod.reshape(depth, 1, n))
    return out.reshape(depth, rows, 6, d)


def _mod_index(seq_len, tm, mod_row):
    if mod_row is None:
        per_seq = seq_len // tm
        return lambda i: (i // per_seq, 0, 0)
    return lambda i: (mod_row, 0, 0)


def _even_in_body(x_ref, mod_ref, nw_ref, w_ref, qw_ref, kw_ref, cos_ref, sin_ref, gmat_ref,
                  qa_ref, ka_ref, va_ref, bq_ref, z_ref, bi_ref, sg_ref, *, use_rope):
    h = _norm_mod(x_ref[...], nw_ref[...], mod_ref[0, 1:2, :], mod_ref[0, 0:1, :]).astype(BF16)
    tm = h.shape[0]
    lane = lax.broadcasted_iota(jnp.int32, (tm, LANES), 1)
    lo_half = (lane % HEAD_DIM) < (HEAD_DIM // 2)
    lo_head = lane < HEAD_DIM
    gmat = gmat_ref[...]

    def proj(a, b):
        return jnp.dot(h, w_ref[:, a:b], preferred_element_type=F32)

    def head_norm_rope(zc, w):
        ms = jnp.dot(jnp.concatenate(_split_bf16(zc * zc, 2), axis=1), gmat, preferred_element_type=F32)
        y = zc * lax.rsqrt(ms + EPS) * w
        if use_rope:
            rot = jnp.where(lo_half, pltpu.roll(y, LANES - HEAD_DIM // 2, 1), pltpu.roll(y, HEAD_DIM // 2, 1))
            y = y * cos_ref[...] + rot * sin_ref[...]
        return y

    zq = proj(0, A_Q_W)
    for c in range(A_Q_W // LANES):
        y = head_norm_rope(zq[:, c * LANES:(c + 1) * LANES], qw_ref[...]) * (ATTN_SCALE * LOG2E)
        swapped = pltpu.roll(y, HEAD_DIM, 1)
        g = (2 * c) // A_GROUP
        if g == 0:
            h0, h1 = jnp.where(lo_head, y, 0.0), jnp.where(lo_head, swapped, 0.0)
        else:
            h0, h1 = jnp.where(lo_head, 0.0, swapped), jnp.where(lo_head, 0.0, y)
        qa_ref[:, (2 * c) * LANES:(2 * c + 1) * LANES] = h0.astype(BF16)
        qa_ref[:, (2 * c + 1) * LANES:(2 * c + 2) * LANES] = h1.astype(BF16)
    o = A_Q_W
    ka_ref[...] = head_norm_rope(proj(o, o + A_KV_W), kw_ref[...]).astype(BF16)
    o += A_KV_W
    va_ref[...] = proj(o, o + A_KV_W).astype(BF16)
    o += A_KV_W
    bq_ref[...] = (_silu(proj(o, o + B_W)) * HGRN_SCALE).astype(BF16)
    o += B_W
    z_ref[...] = proj(o, o + 2 * B_W)
    o += 2 * B_W
    bi_ref[...] = proj(o, o + B_W).astype(BF16)
    o += B_W
    sg_ref[...] = _silu(proj(o, o + B_W)).astype(BF16)


def _even_in(x, modtab, norm_w, w_in, q_norm, k_norm, rope, gmat, *, seq_len, mod_row, tm):
    n, d = x.shape
    use_rope = rope is not None
    cos, sin = rope if use_rope else (jnp.zeros((8, LANES), F32),) * 2
    per_seq = seq_len // tm
    rope_spec = pl.BlockSpec((tm, LANES), lambda i: (i % per_seq, 0)) if use_rope else _const_spec((8, LANES))
    row = lambda w: pl.BlockSpec((tm, w), lambda i: (i, 0))
    widths = (2 * A_Q_W, A_KV_W, A_KV_W, B_W, 2 * B_W, B_W, B_W)
    dtypes = (BF16, BF16, BF16, BF16, F32, BF16, BF16)
    return pl.pallas_call(
        functools.partial(_even_in_body, use_rope=use_rope),
        grid=(n // tm,),
        in_specs=[
            row(d),
            pl.BlockSpec((1, 6, d), _mod_index(seq_len, tm, mod_row)),
            _const_spec((1, d)),
            _layer_spec(*w_in),
            _const_spec((1, LANES)),
            _const_spec((1, LANES)),
            rope_spec, rope_spec,
            _const_spec((2 * LANES, LANES)),
        ],
        out_specs=[row(w) for w in widths],
        out_shape=[jax.ShapeDtypeStruct((n, w), t) for w, t in zip(widths, dtypes)],
        compiler_params=_cparams("parallel"),
        name="even_in",
    )(x, modtab, norm_w.reshape(1, d), w_in[0], q_norm, k_norm, cos, sin, gmat)


def _gqa_body(*refs, tq, n_lat_blk):
    if n_lat_blk:
        q_ref, kc_ref, vc_ref, kl_ref, vl_ref, o_ref, s_ref, m_ref, acc_ref = refs
    else:
        q_ref, kc_ref, vc_ref, o_ref, s_ref, m_ref, acc_ref = refs
    sq = GQA_SUB
    lo_head = lax.broadcasted_iota(jnp.int32, (sq, LANES), 1) < HEAD_DIM
    lo_head_kv = lax.broadcasted_iota(jnp.int32, (KV_BLOCK, LANES), 1) < HEAD_DIM
    units = [(g, sub) for g in range(A_KV_HEADS) for sub in range(tq // sq)]

    def lat_rows(j):
        return pl.ds(pl.multiple_of(j * KV_BLOCK, KV_BLOCK), KV_BLOCK)

    def load_q(g, sub):
        return jnp.concatenate(
            [q_ref[0, sub * sq:(sub + 1) * sq, (A_GROUP * g + j) * LANES:(A_GROUP * g + j + 1) * LANES]
             for j in range(A_GROUP)], axis=0)

    def scores(slot, q, j, k):
        s = lax.dot_general(q, k, (((1,), (1,)), ((), ())), preferred_element_type=F32)
        s_ref[slot, j] = s
        blk_max = s[:, 0:LANES]
        for c in range(1, KV_BLOCK // LANES):
            blk_max = jnp.maximum(blk_max, s[:, c * LANES:(c + 1) * LANES])
        return blk_max

    def probs(slot, j):
        m = m_ref[slot]
        return jnp.exp2(s_ref[slot, j] - jnp.concatenate([m] * (KV_BLOCK // LANES), axis=1)).astype(BF16)

    def weighted(g, p, v):
        own = lo_head_kv if g == 0 else ~lo_head_kv
        own = jnp.concatenate([own] * (v.shape[0] // KV_BLOCK), axis=0)
        return jnp.dot(p, jnp.where(own, v, jnp.ones_like(v)), preferred_element_type=F32)

    def finish(g, sub):
        acc = acc_ref[...]
        o = acc / pltpu.roll(acc, HEAD_DIM, 1)
        for c in range(A_GROUP // 2):
            o0 = o[(2 * c) * sq:(2 * c + 1) * sq]
            o1 = o[(2 * c + 1) * sq:(2 * c + 2) * sq]
            if g == 0:
                pair = jnp.where(lo_head, o0, pltpu.roll(o1, HEAD_DIM, 1))
            else:
                pair = jnp.where(lo_head, pltpu.roll(o0, HEAD_DIM, 1), o1)
            col = (A_GROUP // 2 * g + c) * LANES
            o_ref[0, sub * sq:(sub + 1) * sq, col:col + LANES] = pair.astype(BF16)

    def phase(idx):
        score = units[idx] if idx < len(units) else None
        pv = units[idx - 1] if idx >= 1 else None
        slot_s, slot_p = idx % 2, (idx - 1) % 2
        if score:
            m_ref[slot_s] = scores(slot_s, load_q(*score), 0, kc_ref[0])
        if pv:
            acc_ref[...] = weighted(pv[0], probs(slot_p, 0), vc_ref[0])
        if n_lat_blk:
            def step(i, carry):
                blk_max, ps = None, []
                for u in range(GQA_UNROLL):
                    j = i * GQA_UNROLL + u
                    if score:
                        cur = scores(slot_s, load_q(*score), j + 1, kl_ref[0, lat_rows(j), :])
                        blk_max = cur if blk_max is None else jnp.maximum(blk_max, cur)
                    if pv:
                        ps.append(probs(slot_p, j + 1))
                if score:
                    m_ref[slot_s] = jnp.maximum(m_ref[slot_s], blk_max)
                if pv:
                    span = GQA_UNROLL * KV_BLOCK
                    v = vl_ref[0, pl.ds(pl.multiple_of(i * span, span), span), :]
                    acc_ref[...] += weighted(pv[0], jnp.concatenate(ps, axis=1), v)
                return carry
            lax.fori_loop(0, n_lat_blk // GQA_UNROLL, step, 0)
        if score:
            m_ref[slot_s] = jnp.broadcast_to(jnp.max(m_ref[slot_s], axis=1, keepdims=True), m_ref.shape[1:])
        if pv:
            finish(*pv)

    for idx in range(len(units) + 1):
        phase(idx)


def _gqa(q, k_ctx, v_ctx, k_lat=None, v_lat=None, *, tq):
    b, nq, _ = q.shape
    n_ctx = k_ctx.shape[1]
    assert n_ctx == KV_BLOCK
    with_lat = k_lat is not None
    n_lat = k_lat.shape[1] if with_lat else 0
    kv_spec = lambda n: pl.BlockSpec((1, n, A_KV_W), lambda i, j: (i, 0, 0))
    in_specs = [pl.BlockSpec((1, tq, 2 * A_Q_W), lambda i, j: (i, j, 0)), kv_spec(n_ctx), kv_spec(n_ctx)]
    args = [q, k_ctx, v_ctx]
    if with_lat:
        in_specs += [kv_spec(n_lat), kv_spec(n_lat)]
        args += [k_lat, v_lat]
    rows = A_GROUP * GQA_SUB
    n_blk = 1 + n_lat // KV_BLOCK
    assert (n_blk - 1) % GQA_UNROLL == 0 and tq % GQA_SUB == 0
    return pl.pallas_call(
        functools.partial(_gqa_body, tq=tq, n_lat_blk=n_lat // KV_BLOCK),
        grid=(b, nq // tq),
        in_specs=in_specs,
        out_specs=pl.BlockSpec((1, tq, A_Q_W), lambda i, j: (i, j, 0)),
        out_shape=jax.ShapeDtypeStruct((b, nq, A_Q_W), BF16),
        scratch_shapes=[pltpu.VMEM((2, n_blk, rows, KV_BLOCK), F32), pltpu.VMEM((2, rows, LANES), F32),
                        pltpu.VMEM((rows, LANES), F32)],
        compiler_params=_cparams("parallel", "parallel"),
        name="gqa_lat" if with_lat else "gqa_ctx",
    )(*args)


def _hgrn_masks(rev):
    c = HGRN_CHUNK
    t = lax.broadcasted_iota(jnp.int32, (c, c), 0)
    s = lax.broadcasted_iota(jnp.int32, (c, c), 1)
    causal = (s >= t) if rev else (s <= t)
    diag = ((t // HGRN_SUB) == (s // HGRN_SUB)) & causal
    levels = []
    size = 2 * HGRN_SUB
    while size <= c:
        half = size // 2
        t_hi, s_hi = (t % size) >= half, (s % size) >= half
        pair = (~t_hi & s_hi) if rev else (t_hi & ~s_hi)
        levels.append((size, ((t // size) == (s // size)) & pair))
        size *= 2
    return diag, levels


def _hgrn_chunk(row0, rev, q_ref, z_ref, v_ref, lb_ref, tri_ref, sel_ref, o_ref, st_ref, cbuf, bbuf, pcat):
    c, sub = HGRN_CHUNK, HGRN_SUB
    rows = pl.ds(row0, c)
    diag_mask, level_masks = _hgrn_masks(rev)
    z = z_ref[0, rows, :]
    log_lb, log_1m_lb = lb_ref[0, 0:1, :], lb_ref[0, 1:2, :]
    log_sig = jnp.minimum(z, 0.0) - jnp.log(1.0 + jnp.exp(-jnp.abs(z)))
    cterm = log_1m_lb + log_sig
    log_f = jnp.maximum(log_lb, cterm) + jnp.log(1.0 + jnp.exp(-jnp.abs(log_lb - cterm)))
    lk_all = (cterm - z) * LOG2E
    b_all = jnp.dot(tri_ref[...], jnp.concatenate(_split_bf16(log_f * LOG2E, 3), axis=0),
                    preferred_element_type=F32)
    per_head = []
    for hd in range(B_HEADS):
        cols = slice(hd * LANES, (hd + 1) * LANES)
        q = q_ref[0, rows, cols].astype(F32)
        lk, b = lk_all[:, cols], b_all[:, cols]
        cbuf[hd] = b - lk
        bbuf[hd] = b
        for u in range(sub):
            cu = jnp.concatenate(
                [jnp.broadcast_to(cbuf[hd, m * sub + u:m * sub + u + 1, :], (sub, LANES)) for m in range(c // sub)], 0)
            pcat[hd * c:(hd + 1) * c, u * LANES:(u + 1) * LANES] = (
                q * jnp.exp2(jnp.minimum(b - cu, 0.0))).astype(BF16)
        per_head.append((q, lk, b))
    a_diag = jnp.dot(pcat[...], sel_ref[...], preferred_element_type=F32)
    for hd in range(B_HEADS):
        cols = slice(hd * LANES, (hd + 1) * LANES)
        q, lk, b = per_head[hd]
        v = v_ref[0, rows, cols]
        a = jnp.where(diag_mask, a_diag[hd * c:(hd + 1) * c], 0.0)
        for size, mask in level_masks:
            half = size // 2
            ref_rows = [m * size + (half if rev else half - 1) for m in range(c // size)]
            beta = jnp.concatenate(
                [jnp.broadcast_to(bbuf[hd, r:r + 1, :], (size, LANES)) for r in ref_rows], 0)
            e = -jnp.abs(b - beta)
            lvl = lax.dot_general((q * jnp.exp2(e)).astype(BF16), jnp.exp2(e + lk).astype(BF16),
                                  (((1,), (1,)), ((), ())), preferred_element_type=F32)
            a = a + jnp.where(mask, lvl, 0.0)
        b_end = bbuf[hd, 0:1, :] if rev else bbuf[hd, c - 1:c, :]
        st = st_ref[0, 0, hd]
        o = jnp.dot(a.astype(BF16), v, preferred_element_type=F32)
        o = o + lax.dot_general((q * jnp.exp2(b)).astype(BF16), st.astype(BF16), (((1,), (1,)), ((), ())),
                                preferred_element_type=F32)
        o_ref[0, 0, rows, cols] = o
        kd = jnp.exp2(lk + (b_end - b)).astype(BF16)
        upd = lax.dot_general(v, kd, (((0,), (0,)), ((), ())), preferred_element_type=F32)
        st_ref[0, 0, hd] = st * jnp.exp2(b_end) + upd


def _hgrn_body(q_ref, z_ref, v_ref, lb_ref, tri_ref, sel_ref, s0_ref, o_ref, st_ref, kbuf, bbuf, pcat):
    d, j = pl.program_id(1), pl.program_id(2)
    n_chunks = HGRN_BLOCK // HGRN_CHUNK

    @pl.when(j == 0)
    def _():
        st_ref[...] = s0_ref[...]

    for rev in (False, True):
        @pl.when(d == int(rev))
        def _(rev=rev):
            for i in range(n_chunks):
                ci = (n_chunks - 1 - i) if rev else i
                _hgrn_chunk(ci * HGRN_CHUNK, rev, q_ref, z_ref, v_ref, lb_ref, tri_ref, sel_ref, o_ref, st_ref,
                            kbuf.at[ci], bbuf.at[ci], pcat.at[ci])


def _hgrn(q, z, v, lbtab, tri, sel, s0):
    b, n, _ = q.shape
    nblk = n // HGRN_BLOCK
    c = HGRN_CHUNK

    def blk(d, j):
        return jnp.where(d == 0, j, nblk - 1 - j)

    tok = lambda i, d, j: (i, blk(d, j), 0)
    st_spec = pl.BlockSpec((1, 1, B_HEADS, LANES, LANES), lambda i, d, j: (i, d, 0, 0, 0))
    return pl.pallas_call(
        _hgrn_body,
        grid=(b, 2, nblk),
        in_specs=[
            pl.BlockSpec((1, HGRN_BLOCK, B_W), tok),
            pl.BlockSpec((1, HGRN_BLOCK, B_W), lambda i, d, j: (i, blk(d, j), d)),
            pl.BlockSpec((1, HGRN_BLOCK, B_W), tok),
            pl.BlockSpec((1, 8, B_W), lambda i, d, j: (d, 0, 0)),
            pl.BlockSpec((None, c, 3 * c), lambda i, d, j: (d, 0, 0)),
            _const_spec(sel.shape),
            st_spec,
        ],
        out_specs=[
            pl.BlockSpec((1, 1, HGRN_BLOCK, B_W), lambda i, d, j: (d, i, blk(d, j), 0)),
            st_spec,
        ],
        out_shape=[jax.ShapeDtypeStruct((2, b, n, B_W), F32),
                   jax.ShapeDtypeStruct(s0.shape, F32)],
        scratch_shapes=[pltpu.VMEM((HGRN_BLOCK // c, B_HEADS, c, LANES), F32),
                        pltpu.VMEM((HGRN_BLOCK // c, B_HEADS, c, LANES), F32),
                        pltpu.VMEM((HGRN_BLOCK // c, B_HEADS * c, HGRN_SUB * LANES), BF16)],
        compiler_params=_cparams("parallel", "parallel", "arbitrary"),
        name="hgrn_scan",
    )(q, z, v, lbtab, tri, sel, s0)


def _odd_in_body(x_ref, mod_ref, nw_ref, w_ref, q_ref, k_ref, v_ref):
    h = _norm_mod(x_ref[...], nw_ref[...], mod_ref[0, 1:2, :], mod_ref[0, 0:1, :]).astype(BF16)
    cw = q_ref.shape[1]
    q_ref[...] = (jnp.dot(h, w_ref[:, 0:cw], preferred_element_type=F32) * ATTN_SCALE).astype(BF16)
    k_ref[...] = jnp.dot(h, w_ref[:, cw:2 * cw], preferred_element_type=F32).astype(BF16)
    v_ref[...] = jnp.dot(h, w_ref[:, 2 * cw:3 * cw], preferred_element_type=F32).astype(BF16)


def _odd_in(x, modtab, norm_w, w_qkv, *, seq_len, mod_row, tm):
    n, d = x.shape
    cw = w_qkv[0].shape[2] // 3
    row = pl.BlockSpec((tm, cw), lambda i: (i, 0))
    return pl.pallas_call(
        _odd_in_body,
        grid=(n // tm,),
        in_specs=[
            pl.BlockSpec((tm, d), lambda i: (i, 0)),
            pl.BlockSpec((1, 6, d), _mod_index(seq_len, tm, mod_row)),
            _const_spec((1, d)),
            _layer_spec(*w_qkv),
        ],
        out_specs=[row, row, row],
        out_shape=[jax.ShapeDtypeStruct((n, cw), BF16)] * 3,
        compiler_params=_cparams("parallel"),
        name="odd_in",
    )(x, modtab, norm_w.reshape(1, d), w_qkv[0])


def _na_body(ql_ref, kl_ref, vl_ref, qc_ref, kc_ref, vc_ref, bias_ref, ol_ref, oc_ref, *, rows):
    band = NA_ROWS * GRID_W
    lane_q = lax.broadcasted_iota(jnp.int32, (GRID_W, LANES), 1) < HEAD_DIM
    kc, vc = kc_ref[0], vc_ref[0]
    nt = (((1,), (1,)), ((), ()))

    def two_heads(q, lo_mask):
        zero = jnp.zeros_like(q)
        return jnp.concatenate([jnp.where(lo_mask, q, zero), jnp.where(lo_mask, zero, q)], axis=0)

    def merge(o, n, lo_mask):
        return jnp.where(lo_mask, o[:n], o[n:])

    def with_ones(v):
        return jnp.concatenate([v, jnp.ones_like(v)], axis=1)

    vc_aug = with_ones(vc)

    def row_group(i, carry):
        n2 = 2 * GRID_W
        rs = [i * NA_ROW_UNROLL + u for u in range(NA_ROW_UNROLL)]
        q = jnp.concatenate(
            [two_heads(ql_ref[0, pl.ds(pl.multiple_of(r * GRID_W, GRID_W), GRID_W), :], lane_q) for r in rs], axis=0)
        s_ctx = lax.dot_general(q, kc, nt, preferred_element_type=F32)
        p_ctx, p_loc, v_band = [], [], []
        for u, r in enumerate(rs):
            r0 = jnp.clip(r - NA_ROWS // 2, 0, rows - NA_ROWS)
            kstart = pl.multiple_of(r0 * GRID_W, GRID_W)
            s_loc = lax.dot_general(q[u * n2:(u + 1) * n2], kl_ref[0, pl.ds(kstart, band), :], nt,
                                    preferred_element_type=F32) + bias_ref[0, r - r0]
            s_c = s_ctx[u * n2:(u + 1) * n2]
            m = jnp.maximum(jnp.max(s_loc, axis=1, keepdims=True), jnp.max(s_c, axis=1, keepdims=True))
            p_loc.append(jnp.exp(s_loc - m).astype(BF16))
            p_ctx.append(jnp.exp(s_c - m).astype(BF16))
            v_band.append(vl_ref[0, pl.ds(kstart, band), :])
        o_ctx = jnp.dot(jnp.concatenate(p_ctx, axis=0), vc_aug, preferred_element_type=F32)
        for u, r in enumerate(rs):
            o = jnp.dot(p_loc[u], with_ones(v_band[u]), preferred_element_type=F32) + o_ctx[u * n2:(u + 1) * n2]
            o = o[:, :LANES] / o[:, LANES:]
            ol_ref[0, pl.ds(pl.multiple_of(r * GRID_W, GRID_W), GRID_W), :] = (
                merge(o, GRID_W, lane_q).astype(BF16))
        return carry

    lax.fori_loop(0, rows // NA_ROW_UNROLL, row_group, 0)

    n_ctx = qc_ref.shape[1]
    lane_c = lax.broadcasted_iota(jnp.int32, (n_ctx, LANES), 1) < HEAD_DIM
    q = two_heads(qc_ref[0], lane_c)
    s = lax.dot_general(q, kc, nt, preferred_element_type=F32)
    p = jnp.exp(s - jnp.max(s, axis=1, keepdims=True)).astype(BF16)
    o = jnp.dot(p, vc_aug, preferred_element_type=F32)
    oc_ref[0] = merge(o[:, :LANES] / o[:, LANES:], n_ctx, lane_c).astype(BF16)


def _na_bias(rpb):
    heads, n_dy, n_dx = rpb.shape
    qc = np.arange(GRID_W)
    kc = np.arange(GRID_W)
    c_start = np.clip(qc - NA_COLS // 2, 0, GRID_W - NA_COLS)
    in_win = (kc[None, :] >= c_start[:, None]) & (kc[None, :] < c_start[:, None] + NA_COLS)
    dx = kc[None, :] - qc[:, None] + (NA_COLS - 1)
    onehot = ((dx[:, :, None] == np.arange(n_dx)[None, None, :]) & in_win[:, :, None]).astype(np.float32)
    t = jnp.einsum("hyx,qcx->hqyc", rpb.astype(F32), jnp.asarray(onehot), precision=HIGHEST)
    t = jnp.where(jnp.asarray(in_win)[None, :, None, :], t, NEG_BIG).reshape(heads, GRID_W, n_dy * GRID_W)
    width = NA_ROWS * GRID_W
    tab = jnp.stack([t[:, :, (NA_ROWS - 1 - var) * GRID_W:(NA_ROWS - 1 - var) * GRID_W + width]
                     for var in range(NA_ROWS)], axis=1)
    tab = tab.reshape(heads // 2, 2, NA_ROWS, GRID_W, width)
    return tab.transpose(0, 2, 1, 3, 4).reshape(heads // 2, NA_ROWS, 2 * GRID_W, width)


def _na(q_lat, k_lat, v_lat, q_ctx, k_ctx, v_ctx, bias):
    b, n, w = q_lat.shape
    n_ctx = q_ctx.shape[1]
    rows = n // GRID_W
    assert rows >= NA_ROWS and rows % NA_ROW_UNROLL == 0
    pairs = w // LANES
    lat = pl.BlockSpec((1, n, LANES), lambda p, i: (i, 0, p))
    ctx = pl.BlockSpec((1, n_ctx, LANES), lambda p, i: (i, 0, p))
    return pl.pallas_call(
        functools.partial(_na_body, rows=rows),
        grid=(pairs, b),
        in_specs=[lat, lat, lat, ctx, ctx, ctx,
                  pl.BlockSpec((1,) + bias.shape[1:], lambda p, i: (p, 0, 0, 0))],
        out_specs=[lat, ctx],
        out_shape=[jax.ShapeDtypeStruct((b, n, w), BF16), jax.ShapeDtypeStruct((b, n_ctx, w), BF16)],
        compiler_params=_cparams("parallel", "parallel"),
        name="na_attn",
    )(q_lat, k_lat, v_lat, q_ctx, k_ctx, v_ctx, bias)


def _post(acc, post_w, gate, x):
    return x + gate * _rms(acc, post_w)


def _even_out_body(x_ref, mod_ref, oa_ref, ob_ref, sg_ref, onw_ref, w_ref, pw_ref, y_ref):
    acc = jnp.dot(oa_ref[...], w_ref[0:A_Q_W, :], preferred_element_type=F32)
    ob = ob_ref[0] + ob_ref[1]
    parts = []
    for hd in range(B_HEADS):
        cols = slice(hd * LANES, (hd + 1) * LANES)
        parts.append((_rms(ob[:, cols], onw_ref[...]) * sg_ref[:, cols].astype(F32)).astype(BF16))
    acc = acc + jnp.dot(jnp.concatenate(parts, axis=1), w_ref[A_Q_W:, :], preferred_element_type=F32)
    y_ref[...] = _post(acc, pw_ref[...], mod_ref[0, 2:3, :], x_ref[...])


def _even_out(x, modtab, oa, ob, sg, out_norm, w_out, post_w, *, seq_len, mod_row, tm):
    n, d = x.shape
    row = lambda w: pl.BlockSpec((tm, w), lambda i: (i, 0))
    return pl.pallas_call(
        _even_out_body,
        grid=(n // tm,),
        in_specs=[
            row(d),
            pl.BlockSpec((1, 6, d), _mod_index(seq_len, tm, mod_row)),
            row(A_Q_W),
            pl.BlockSpec((2, tm, B_W), lambda i: (0, i, 0)),
            row(B_W),
            _const_spec((1, LANES)),
            _layer_spec(*w_out),
            _const_spec((1, d)),
        ],
        out_specs=row(d),
        out_shape=jax.ShapeDtypeStruct((n, d), F32),
        compiler_params=_cparams("parallel"),
        name="even_out",
    )(x, modtab, oa, ob, sg, out_norm.reshape(1, LANES), w_out[0], post_w.reshape(1, d))


def _odd_out_body(x_ref, mod_ref, o_ref, w_ref, pw_ref, y_ref):
    acc = jnp.dot(o_ref[...], w_ref[...], preferred_element_type=F32)
    y_ref[...] = _post(acc, pw_ref[...], mod_ref[0, 2:3, :], x_ref[...])


def _odd_out(x, modtab, o, w_out, post_w, *, seq_len, mod_row, tm):
    n, d = x.shape
    row = lambda w: pl.BlockSpec((tm, w), lambda i: (i, 0))
    return pl.pallas_call(
        _odd_out_body,
        grid=(n // tm,),
        in_specs=[
            row(d),
            pl.BlockSpec((1, 6, d), _mod_index(seq_len, tm, mod_row)),
            row(o.shape[1]),
            _layer_spec(*w_out),
            _const_spec((1, d)),
        ],
        out_specs=row(d),
        out_shape=jax.ShapeDtypeStruct((n, d), F32),
        compiler_params=_cparams("parallel"),
        name="odd_out",
    )(x, modtab, o, w_out[0], post_w.reshape(1, d))


FFN_HALO = BF16_SUBLANES
FFN_TN = 256
FFN_TM = 1024


def _ffn_body(x_ref, xp_ref, xn_ref, mod_ref, nw_ref, wup_ref, cw_ref, cb_ref, wdn_ref, pw_ref, y_ref,
              hbuf, ua, ug, act, *, seq_len):
    tm, d = x_ref.shape
    dff = wdn_ref.shape[0]
    halo = FFN_HALO
    scale, shift = mod_ref[0, 4:5, :], mod_ref[0, 3:4, :]
    x = x_ref[...]
    row0 = pl.program_id(0) * tm
    top_in = (row0 % seq_len) != 0
    bottom_in = ((row0 + tm) % seq_len) != 0
    hbuf[0:halo, :] = jnp.where(top_in, _norm_mod(xp_ref[...], nw_ref[...], scale, shift), 0.0).astype(BF16)
    hbuf[halo:halo + tm, :] = _norm_mod(x, nw_ref[...], scale, shift).astype(BF16)
    hbuf[halo + tm:, :] = jnp.where(bottom_in, _norm_mod(xn_ref[...], nw_ref[...], scale, shift), 0.0).astype(BF16)
    h = hbuf[...]
    for j in range(dff // FFN_TN):
        halves = []
        for buf, base in ((ua, 0), (ug, dff)):
            cols = slice(base + j * FFN_TN, base + (j + 1) * FFN_TN)
            buf[...] = jnp.dot(h, wup_ref[:, cols], preferred_element_type=F32)
            halves.append(
                cw_ref[0:1, cols] * buf[halo - 1:halo - 1 + tm, :]
                + cw_ref[1:2, cols] * buf[halo:halo + tm, :]
                + cw_ref[2:3, cols] * buf[halo + 1:halo + 1 + tm, :]
                + cb_ref[0:1, cols])
        a, g = halves
        act[:, j * FFN_TN:(j + 1) * FFN_TN] = (_silu(g) * a).astype(BF16)
    out = jnp.dot(act[...], wdn_ref[...], preferred_element_type=F32)
    y_ref[...] = _post(out, pw_ref[...], mod_ref[0, 5:6, :], x)


def _ffn(x, modtab, norm_w, w_up, conv_w, conv_b, w_down, post_w, *, seq_len, mod_row, tm):
    n, d = x.shape
    dff = w_down[0].shape[1]
    assert dff % FFN_TN == 0 and tm % FFN_HALO == 0 and seq_len % tm == 0
    per_tile = tm // FFN_HALO
    last = n // FFN_HALO - 1
    return pl.pallas_call(
        functools.partial(_ffn_body, seq_len=seq_len),
        grid=(n // tm,),
        in_specs=[
            pl.BlockSpec((tm, d), lambda i: (i, 0)),
            pl.BlockSpec((FFN_HALO, d), lambda i: (jnp.maximum(i * per_tile - 1, 0), 0)),
            pl.BlockSpec((FFN_HALO, d), lambda i: (jnp.minimum((i + 1) * per_tile, last), 0)),
            pl.BlockSpec((1, 6, d), _mod_index(seq_len, tm, mod_row)),
            _const_spec((1, d)),
            _layer_spec(*w_up),
            _const_spec(conv_w.shape),
            _const_spec((1, 2 * dff)),
            _layer_spec(*w_down),
            _const_spec((1, d)),
        ],
        out_specs=pl.BlockSpec((tm, d), lambda i: (i, 0)),
        out_shape=jax.ShapeDtypeStruct((n, d), F32),
        scratch_shapes=[pltpu.VMEM((tm + 2 * FFN_HALO, d), BF16),
                        pltpu.VMEM((tm + 2 * FFN_HALO, FFN_TN), F32),
                        pltpu.VMEM((tm + 2 * FFN_HALO, FFN_TN), F32),
                        pltpu.VMEM((tm, dff), BF16)],
        compiler_params=_cparams("parallel"),
        name="conv_ffn",
    )(x, x, x, modtab, norm_w.reshape(1, d), w_up[0], conv_w, conv_b.reshape(1, 2 * dff), w_down[0],
      post_w.reshape(1, d))


def _rope_tables(n_tokens):
    t = jnp.arange(n_tokens)
    row = (t // GRID_W).astype(F32)
    col = (t % GRID_W).astype(F32)
    n_freq = HEAD_DIM // 4
    inv_freq = ROPE_THETA ** (-jnp.arange(n_freq, dtype=F32) / n_freq)
    ang = jnp.concatenate([row[:, None] * inv_freq, col[:, None] * inv_freq], axis=-1)
    cos, sin = jnp.cos(ang), jnp.sin(ang)
    cos_t = jnp.tile(cos, (1, LANES // (HEAD_DIM // 2)))
    sin_t = jnp.tile(jnp.concatenate([-sin, sin], axis=-1), (1, LANES // HEAD_DIM))
    return cos_t, sin_t


def _hgrn_tables():
    c = HGRN_CHUNK
    lower = jnp.tril(jnp.ones((c, c), F32))
    tri = jnp.stack([lower, lower.T])
    tri = jnp.concatenate([tri] * 3, axis=2).astype(BF16)
    u = jnp.repeat(jnp.arange(HGRN_SUB), LANES)
    sel = (u[:, None] == (jnp.arange(c) % HGRN_SUB)[None, :]).astype(BF16)
    return tri, sel


def _lb_table(lb):
    pad = jnp.zeros((2, 6, lb.shape[-1]), F32)
    return jnp.concatenate([jnp.log(lb)[:, None], jnp.log1p(-lb)[:, None], pad], axis=1)


def kernel(x, c, ctx, c_ctx, w_mod, b_mod, norm_pre_mix, norm_post_mix, norm_pre_ffn, norm_post_ffn, even_w_in, even_w_out, even_q_norm, even_k_norm, hgrn_lb_logits, hgrn_out_norm, odd_w_qkv, odd_w_out, odd_rpb, ffn_w_up, ffn_conv_w, ffn_conv_b, ffn_w_down):
    bsz, seq, d = x.shape
    n_ctx = ctx.shape[1]
    depth = w_mod.shape[0]
    n_even = even_w_in.shape[0]

    modtab = _modulation(c, c_ctx, w_mod, b_mod)
    even_w_in, even_w_out, odd_w_qkv, odd_w_out, ffn_w_up, ffn_w_down = (
        _to_bf16(w) for w in (even_w_in, even_w_out, odd_w_qkv, odd_w_out, ffn_w_up, ffn_w_down))
    ctx_row = bsz
    rope = _rope_tables(seq)
    gmat = jnp.kron(jnp.eye(LANES // HEAD_DIM, dtype=F32), jnp.full((HEAD_DIM, HEAD_DIM), 1.0 / HEAD_DIM, F32))
    gmat = jnp.concatenate([gmat, gmat], axis=0).astype(BF16)
    tri, sel = _hgrn_tables()
    p_lb = jax.nn.softmax(hgrn_lb_logits.astype(F32), axis=0)
    lb_all = jnp.concatenate([jnp.zeros_like(p_lb[:1]), jnp.cumsum(p_lb[1:], axis=0)], axis=0)

    xl = x.reshape(bsz * seq, d)
    xc = ctx.reshape(bsz * n_ctx, d)
    tm_lat = min(512, seq)
    tm_ctx = min(512, bsz * n_ctx)
    lat_kw = dict(seq_len=seq, mod_row=None, tm=tm_lat)
    ctx_kw = dict(seq_len=n_ctx, mod_row=ctx_row, tm=tm_ctx)

    for l in range(depth):
        with_ctx_out = l < depth - 1
        mt = modtab[l]
        if l % 2 == 0:
            e = l // 2
            w_in = (even_w_in, e)
            w_out = (even_w_out, e)
            qn = jnp.tile(even_q_norm[e], LANES // HEAD_DIM).reshape(1, LANES)
            kn = jnp.tile(even_k_norm[e], LANES // HEAD_DIM).reshape(1, LANES)
            pl_ = _even_in(xl, mt, norm_pre_mix[l], w_in, qn, kn, rope, gmat, **lat_kw)
            pc_ = _even_in(xc, mt, norm_pre_mix[l], w_in, qn, kn, None, gmat, **ctx_kw)
            qa_l, ka_l, va_l, bq_l, z_l, bi_l, sg_l = pl_
            qa_c, ka_c, va_c, bq_c, z_c, bi_c, sg_c = pc_
            r3 = lambda a, n: a.reshape(bsz, n, a.shape[-1])
            oa_l = _gqa(r3(qa_l, seq), r3(ka_c, n_ctx), r3(va_c, n_ctx), r3(ka_l, seq), r3(va_l, seq),
                        tq=min(GQA_TQ, seq))
            lbtab = _lb_table(lb_all[e])
            s0 = jnp.zeros((bsz, 2, B_HEADS, LANES, LANES), F32)
            ob_c, s_ctx = _hgrn(r3(bq_c, n_ctx), r3(z_c, n_ctx), r3(bi_c, n_ctx), lbtab, tri, sel, s0)
            ob_l, _ = _hgrn(r3(bq_l, seq), r3(z_l, seq), r3(bi_l, seq), lbtab, tri, sel, s_ctx)
            xl_new = _even_out(xl, mt, oa_l.reshape(bsz * seq, A_Q_W), ob_l.reshape(2, bsz * seq, B_W), sg_l,
                               hgrn_out_norm[e], w_out, norm_post_mix[l], **lat_kw)
            if with_ctx_out:
                oa_c = _gqa(r3(qa_c, n_ctx), r3(ka_c, n_ctx), r3(va_c, n_ctx), tq=n_ctx)
                xc = _even_out(xc, mt, oa_c.reshape(bsz * n_ctx, A_Q_W), ob_c.reshape(2, bsz * n_ctx, B_W), sg_c,
                               hgrn_out_norm[e], w_out, norm_post_mix[l], **ctx_kw)
            xl = xl_new
        else:
            o = l // 2
            w_qkv = (odd_w_qkv, o)
            w_out = (odd_w_out, o)
            ql, kl, vl = _odd_in(xl, mt, norm_pre_mix[l], w_qkv, **lat_kw)
            qc, kc, vc = _odd_in(xc, mt, norm_pre_mix[l], w_qkv, **ctx_kw)
            r3 = lambda a, n: a.reshape(bsz, n, a.shape[-1])
            o_l, o_c = _na(r3(ql, seq), r3(kl, seq), r3(vl, seq), r3(qc, n_ctx), r3(kc, n_ctx), r3(vc, n_ctx),
                           _na_bias(odd_rpb[o]))
            xl = _odd_out(xl, mt, o_l.reshape(bsz * seq, -1), w_out, norm_post_mix[l], **lat_kw)
            if with_ctx_out:
                xc = _odd_out(xc, mt, o_c.reshape(bsz * n_ctx, -1), w_out, norm_post_mix[l], **ctx_kw)

        w_up = (ffn_w_up, l)
        w_dn = (ffn_w_down, l)
        ffn_args = (norm_pre_ffn[l], w_up, ffn_conv_w[l], ffn_conv_b[l], w_dn, norm_post_ffn[l])
        xl = _ffn(xl, mt, *ffn_args, seq_len=seq, mod_row=None, tm=min(FFN_TM, seq))
        if with_ctx_out:
            xc = _ffn(xc, mt, *ffn_args, seq_len=n_ctx, mod_row=ctx_row, tm=n_ctx)
    return xl.reshape(bsz, seq, d)
```

```python
import functools

import numpy as np
import jax
import jax.numpy as jnp
from jax import lax
from jax.experimental import pallas as pl
from jax.experimental.pallas import tpu as pltpu

F32 = jnp.float32
BF16 = jnp.bfloat16
HIGHEST = lax.Precision.HIGHEST

V7X_VMEM_BYTES = 64 * 1024 * 1024
VMEM_LIMIT_BYTES = V7X_VMEM_BYTES - 8 * 1024 * 1024
LANES = 128
BF16_SUBLANES = 16

GRID_W = 64
HEAD_DIM = 64
A_Q_HEADS = 8
A_KV_HEADS = 2
A_GROUP = A_Q_HEADS // A_KV_HEADS
B_HEADS = 4
B_KEY_DIM = 128
C_HEADS = 16
NA_ROWS = 8
NA_COLS = 16
EPS = 1e-6
ROPE_THETA = 10000.0
ATTN_SCALE = HEAD_DIM ** -0.5
HGRN_SCALE = B_KEY_DIM ** -0.5
NEG_BIG = -1e30
LOG2E = 1.4426950408889634

A_Q_W = A_Q_HEADS * HEAD_DIM
A_KV_W = A_KV_HEADS * HEAD_DIM
B_W = B_HEADS * B_KEY_DIM

HGRN_CHUNK = 64
HGRN_SUB = 8
HGRN_BLOCK = 512
KV_BLOCK = 256
GQA_TQ = 1024
GQA_SUB = 128
GQA_UNROLL = 4
NA_ROW_UNROLL = 16


def _cparams(*sem):
    return pltpu.CompilerParams(dimension_semantics=sem, vmem_limit_bytes=VMEM_LIMIT_BYTES)


def _const_spec(shape):
    n = len(shape)
    return pl.BlockSpec(shape, lambda *_: (0,) * n, pipeline_mode=pl.Buffered(1))


def _layer_spec(stack, layer):
    return pl.BlockSpec((None,) + stack.shape[1:], lambda *_: (layer, 0, 0), pipeline_mode=pl.Buffered(1))


def _silu(x):
    return x / (1.0 + jnp.exp(-x))


def _rms(x, w):
    return x * lax.rsqrt(jnp.mean(x * x, axis=-1, keepdims=True) + EPS) * w


def _norm_mod(x, w, scale, shift):
    return _rms(x, w) * (1.0 + scale) + shift


def _split_bf16(x, parts):
    terms = []
    for _ in range(parts):
        hi = x.astype(BF16)
        terms.append(hi)
        x = x - hi.astype(F32)
    return terms


CAST_ROWS = 512


def _cast_body(x_ref, o_ref):
    o_ref[...] = x_ref[...].astype(BF16)


def _to_bf16(w):
    w2 = w.reshape(-1, w.shape[-1])
    rows, cols = w2.shape
    tr = min(CAST_ROWS, rows)
    assert rows % tr == 0
    spec = pl.BlockSpec((tr, cols), lambda i: (i, 0))
    out = pl.pallas_call(
        _cast_body,
        grid=(rows // tr,),
        in_specs=[spec],
        out_specs=spec,
        out_shape=jax.ShapeDtypeStruct((rows, cols), BF16),
        compiler_params=_cparams("parallel"),
        name="cast_bf16",
    )(w2)
    return out.reshape(w.shape)


def _mod_body(c_ref, w_ref, b_ref, o_ref):
    s = _silu(c_ref[...])
    o_ref[0] = jnp.dot(s, w_ref[0], preferred_element_type=F32, precision=HIGHEST) + b_ref[0]


def _modulation(c, c_ctx, w_mod, b_mod):
    depth, d, n = w_mod.shape
    b = c.shape[0]
    rows = -(-(b + 1) // 8) * 8
    cc = jnp.concatenate([c, c_ctx[None, :], jnp.zeros((rows - b - 1, d), F32)], axis=0)
    tn = n // 4
    out = pl.pallas_call(
        _mod_body,
        grid=(depth, n // tn),
        in_specs=[
            pl.BlockSpec((rows, d), lambda l, j: (0, 0)),
            pl.BlockSpec((1, d, tn), lambda l, j: (l, 0, j)),
            pl.BlockSpec((1, 1, tn), lambda l, j: (l, 0, j)),
        ],
        out_specs=pl.BlockSpec((1, rows, tn), lambda l, j: (l, 0, j)),
        out_shape=jax.ShapeDtypeStruct((depth, rows, n), F32),
        compiler_params=_cparams("parallel", "parallel"),
        name="adaln_mod",
    )(cc, w_mod, b_mod.reshape(depth, 1, n))
    return out.reshape(depth, rows, 6, d)


def _mod_index(seq_len, tm, mod_row):
    if mod_row is None:
        per_seq = seq_len // tm
        return lambda i: (i // per_seq, 0, 0)
    return lambda i: (mod_row, 0, 0)


def _even_in_body(x_ref, mod_ref, nw_ref, w_ref, qw_ref, kw_ref, cos_ref, sin_ref, gmat_ref,
                  qa_ref, ka_ref, va_ref, bq_ref, z_ref, bi_ref, sg_ref, *, use_rope):
    h = _norm_mod(x_ref[...], nw_ref[...], mod_ref[0, 1:2, :], mod_ref[0, 0:1, :]).astype(BF16)
    tm = h.shape[0]
    lane = lax.broadcasted_iota(jnp.int32, (tm, LANES), 1)
    lo_half = (lane % HEAD_DIM) < (HEAD_DIM // 2)
    lo_head = lane < HEAD_DIM
    gmat = gmat_ref[...]

    def proj(a, b):
        return jnp.dot(h, w_ref[:, a:b], preferred_element_type=F32)

    def head_norm_rope(zc, w):
        ms = jnp.dot(jnp.concatenate(_split_bf16(zc * zc, 2), axis=1), gmat, preferred_element_type=F32)
        y = zc * lax.rsqrt(ms + EPS) * w
        if use_rope:
            rot = jnp.where(lo_half, pltpu.roll(y, LANES - HEAD_DIM // 2, 1), pltpu.roll(y, HEAD_DIM // 2, 1))
            y = y * cos_ref[...] + rot * sin_ref[...]
        return y

    zq = proj(0, A_Q_W)
    for c in range(A_Q_W // LANES):
        y = head_norm_rope(zq[:, c * LANES:(c + 1) * LANES], qw_ref[...]) * (ATTN_SCALE * LOG2E)
        swapped = pltpu.roll(y, HEAD_DIM, 1)
        g = (2 * c) // A_GROUP
        if g == 0:
            h0, h1 = jnp.where(lo_head, y, 0.0), jnp.where(lo_head, swapped, 0.0)
        else:
            h0, h1 = jnp.where(lo_head, 0.0, swapped), jnp.where(lo_head, 0.0, y)
        qa_ref[:, (2 * c) * LANES:(2 * c + 1) * LANES] = h0.astype(BF16)
        qa_ref[:, (2 * c + 1) * LANES:(2 * c + 2) * LANES] = h1.astype(BF16)
    o = A_Q_W
    ka_ref[...] = head_norm_rope(proj(o, o + A_KV_W), kw_ref[...]).astype(BF16)
    o += A_KV_W
    va_ref[...] = proj(o, o + A_KV_W).astype(BF16)
    o += A_KV_W
    bq_ref[...] = (_silu(proj(o, o + B_W)) * HGRN_SCALE).astype(BF16)
    o += B_W
    z_ref[...] = proj(o, o + 2 * B_W)
    o += 2 * B_W
    bi_ref[...] = proj(o, o + B_W).astype(BF16)
    o += B_W
    sg_ref[...] = _silu(proj(o, o + B_W)).astype(BF16)


def _even_in(x, modtab, norm_w, w_in, q_norm, k_norm, rope, gmat, *, seq_len, mod_row, tm):
    n, d = x.shape
    use_rope = rope is not None
    cos, sin = rope if use_rope else (jnp.zeros((8, LANES), F32),) * 2
    per_seq = seq_len // tm
    rope_spec = pl.BlockSpec((tm, LANES), lambda i: (i % per_seq, 0)) if use_rope else _const_spec((8, LANES))
    row = lambda w: pl.BlockSpec((tm, w), lambda i: (i, 0))
    widths = (2 * A_Q_W, A_KV_W, A_KV_W, B_W, 2 * B_W, B_W, B_W)
    dtypes = (BF16, BF16, BF16, BF16, F32, BF16, BF16)
    return pl.pallas_call(
        functools.partial(_even_in_body, use_rope=use_rope),
        grid=(n // tm,),
        in_specs=[
            row(d),
            pl.BlockSpec((1, 6, d), _mod_index(seq_len, tm, mod_row)),
            _const_spec((1, d)),
            _layer_spec(*w_in),
            _const_spec((1, LANES)),
            _const_spec((1, LANES)),
            rope_spec, rope_spec,
            _const_spec((2 * LANES, LANES)),
        ],
        out_specs=[row(w) for w in widths],
        out_shape=[jax.ShapeDtypeStruct((n, w), t) for w, t in zip(widths, dtypes)],
        compiler_params=_cparams("parallel"),
        name="even_in",
    )(x, modtab, norm_w.reshape(1, d), w_in[0], q_norm, k_norm, cos, sin, gmat)


def _gqa_body(*refs, tq, n_lat_blk):
    if n_lat_blk:
        q_ref, kc_ref, vc_ref, kl_ref, vl_ref, o_ref, s_ref, m_ref, acc_ref = refs
    else:
        q_ref, kc_ref, vc_ref, o_ref, s_ref, m_ref, acc_ref = refs
    sq = GQA_SUB
    lo_head = lax.broadcasted_iota(jnp.int32, (sq, LANES), 1) < HEAD_DIM
    lo_head_kv = lax.broadcasted_iota(jnp.int32, (KV_BLOCK, LANES), 1) < HEAD_DIM
    units = [(g, sub) for g in range(A_KV_HEADS) for sub in range(tq // sq)]

    def lat_rows(j):
        return pl.ds(pl.multiple_of(j * KV_BLOCK, KV_BLOCK), KV_BLOCK)

    def load_q(g, sub):
        return jnp.concatenate(
            [q_ref[0, sub * sq:(sub + 1) * sq, (A_GROUP * g + j) * LANES:(A_GROUP * g + j + 1) * LANES]
             for j in range(A_GROUP)], axis=0)

    def scores(slot, q, j, k):
        s = lax.dot_general(q, k, (((1,), (1,)), ((), ())), preferred_element_type=F32)
        s_ref[slot, j] = s
        blk_max = s[:, 0:LANES]
        for c in range(1, KV_BLOCK // LANES):
            blk_max = jnp.maximum(blk_max, s[:, c * LANES:(c + 1) * LANES])
        return blk_max

    def probs(slot, j):
        m = m_ref[slot]
        return jnp.exp2(s_ref[slot, j] - jnp.concatenate([m] * (KV_BLOCK // LANES), axis=1)).astype(BF16)

    def weighted(g, p, v):
        own = lo_head_kv if g == 0 else ~lo_head_kv
        own = jnp.concatenate([own] * (v.shape[0] // KV_BLOCK), axis=0)
        return jnp.dot(p, jnp.where(own, v, jnp.ones_like(v)), preferred_element_type=F32)

    def finish(g, sub):
        acc = acc_ref[...]
        o = acc / pltpu.roll(acc, HEAD_DIM, 1)
        for c in range(A_GROUP // 2):
            o0 = o[(2 * c) * sq:(2 * c + 1) * sq]
            o1 = o[(2 * c + 1) * sq:(2 * c + 2) * sq]
            if g == 0:
                pair = jnp.where(lo_head, o0, pltpu.roll(o1, HEAD_DIM, 1))
            else:
                pair = jnp.where(lo_head, pltpu.roll(o0, HEAD_DIM, 1), o1)
            col = (A_GROUP // 2 * g + c) * LANES
            o_ref[0, sub * sq:(sub + 1) * sq, col:col + LANES] = pair.astype(BF16)

    def phase(idx):
        score = units[idx] if idx < len(units) else None
        pv = units[idx - 1] if idx >= 1 else None
        slot_s, slot_p = idx % 2, (idx - 1) % 2
        if score:
            m_ref[slot_s] = scores(slot_s, load_q(*score), 0, kc_ref[0])
        if pv:
            acc_ref[...] = weighted(pv[0], probs(slot_p, 0), vc_ref[0])
        if n_lat_blk:
            def step(i, carry):
                blk_max, ps = None, []
                for u in range(GQA_UNROLL):
                    j = i * GQA_UNROLL + u
                    if score:
                        cur = scores(slot_s, load_q(*score), j + 1, kl_ref[0, lat_rows(j), :])
                        blk_max = cur if blk_max is None else jnp.maximum(blk_max, cur)
                    if pv:
                        ps.append(probs(slot_p, j + 1))
                if score:
                    m_ref[slot_s] = jnp.maximum(m_ref[slot_s], blk_max)
                if pv:
                    span = GQA_UNROLL * KV_BLOCK
                    v = vl_ref[0, pl.ds(pl.multiple_of(i * span, span), span), :]
                    acc_ref[...] += weighted(pv[0], jnp.concatenate(ps, axis=1), v)
                return carry
            lax.fori_loop(0, n_lat_blk // GQA_UNROLL, step, 0)
        if score:
            m_ref[slot_s] = jnp.broadcast_to(jnp.max(m_ref[slot_s], axis=1, keepdims=True), m_ref.shape[1:])
        if pv:
            finish(*pv)

    for idx in range(len(units) + 1):
        phase(idx)


def _gqa(q, k_ctx, v_ctx, k_lat=None, v_lat=None, *, tq):
    b, nq, _ = q.shape
    n_ctx = k_ctx.shape[1]
    assert n_ctx == KV_BLOCK
    with_lat = k_lat is not None
    n_lat = k_lat.shape[1] if with_lat else 0
    kv_spec = lambda n: pl.BlockSpec((1, n, A_KV_W), lambda i, j: (i, 0, 0))
    in_specs = [pl.BlockSpec((1, tq, 2 * A_Q_W), lambda i, j: (i, j, 0)), kv_spec(n_ctx), kv_spec(n_ctx)]
    args = [q, k_ctx, v_ctx]
    if with_lat:
        in_specs += [kv_spec(n_lat), kv_spec(n_lat)]
        args += [k_lat, v_lat]
    rows = A_GROUP * GQA_SUB
    n_blk = 1 + n_lat // KV_BLOCK
    assert (n_blk - 1) % GQA_UNROLL == 0 and tq % GQA_SUB == 0
    return pl.pallas_call(
        functools.partial(_gqa_body, tq=tq, n_lat_blk=n_lat // KV_BLOCK),
        grid=(b, nq // tq),
        in_specs=in_specs,
        out_specs=pl.BlockSpec((1, tq, A_Q_W), lambda i, j: (i, j, 0)),
        out_shape=jax.ShapeDtypeStruct((b, nq, A_Q_W), BF16),
        scratch_shapes=[pltpu.VMEM((2, n_blk, rows, KV_BLOCK), F32), pltpu.VMEM((2, rows, LANES), F32),
                        pltpu.VMEM((rows, LANES), F32)],
        compiler_params=_cparams("parallel", "parallel"),
        name="gqa_lat" if with_lat else "gqa_ctx",
    )(*args)


def _hgrn_masks(rev):
    c = HGRN_CHUNK
    t = lax.broadcasted_iota(jnp.int32, (c, c), 0)
    s = lax.broadcasted_iota(jnp.int32, (c, c), 1)
    causal = (s >= t) if rev else (s <= t)
    diag = ((t // HGRN_SUB) == (s // HGRN_SUB)) & causal
    levels = []
    size = 2 * HGRN_SUB
    while size <= c:
        half = size // 2
        t_hi, s_hi = (t % size) >= half, (s % size) >= half
        pair = (~t_hi & s_hi) if rev else (t_hi & ~s_hi)
        levels.append((size, ((t // size) == (s // size)) & pair))
        size *= 2
    return diag, levels


def _hgrn_chunk(row0, rev, q_ref, z_ref, v_ref, lb_ref, tri_ref, sel_ref, o_ref, st_ref, cbuf, bbuf, pcat):
    c, sub = HGRN_CHUNK, HGRN_SUB
    rows = pl.ds(row0, c)
    diag_mask, level_masks = _hgrn_masks(rev)
    z = z_ref[0, rows, :]
    log_lb, log_1m_lb = lb_ref[0, 0:1, :], lb_ref[0, 1:2, :]
    log_sig = jnp.minimum(z, 0.0) - jnp.log(1.0 + jnp.exp(-jnp.abs(z)))
    cterm = log_1m_lb + log_sig
    log_f = jnp.maximum(log_lb, cterm) + jnp.log(1.0 + jnp.exp(-jnp.abs(log_lb - cterm)))
    lk_all = (cterm - z) * LOG2E
    b_all = jnp.dot(tri_ref[...], jnp.concatenate(_split_bf16(log_f * LOG2E, 3), axis=0),
                    preferred_element_type=F32)
    per_head = []
    for hd in range(B_HEADS):
        cols = slice(hd * LANES, (hd + 1) * LANES)
        q = q_ref[0, rows, cols].astype(F32)
        lk, b = lk_all[:, cols], b_all[:, cols]
        cbuf[hd] = b - lk
        bbuf[hd] = b
        for u in range(sub):
            cu = jnp.concatenate(
                [jnp.broadcast_to(cbuf[hd, m * sub + u:m * sub + u + 1, :], (sub, LANES)) for m in range(c // sub)], 0)
            pcat[hd * c:(hd + 1) * c, u * LANES:(u + 1) * LANES] = (
                q * jnp.exp2(jnp.minimum(b - cu, 0.0))).astype(BF16)
        per_head.append((q, lk, b))
    a_diag = jnp.dot(pcat[...], sel_ref[...], preferred_element_type=F32)
    for hd in range(B_HEADS):
        cols = slice(hd * LANES, (hd + 1) * LANES)
        q, lk, b = per_head[hd]
        v = v_ref[0, rows, cols]
        a = jnp.where(diag_mask, a_diag[hd * c:(hd + 1) * c], 0.0)
        for size, mask in level_masks:
            half = size // 2
            ref_rows = [m * size + (half if rev else half - 1) for m in range(c // size)]
            beta = jnp.concatenate(
                [jnp.broadcast_to(bbuf[hd, r:r + 1, :], (size, LANES)) for r in ref_rows], 0)
            e = -jnp.abs(b - beta)
            lvl = lax.dot_general((q * jnp.exp2(e)).astype(BF16), jnp.exp2(e + lk).astype(BF16),
                                  (((1,), (1,)), ((), ())), preferred_element_type=F32)
            a = a + jnp.where(mask, lvl, 0.0)
        b_end = bbuf[hd, 0:1, :] if rev else bbuf[hd, c - 1:c, :]
        st = st_ref[0, 0, hd]
        o = jnp.dot(a.astype(BF16), v, preferred_element_type=F32)
        o = o + lax.dot_general((q * jnp.exp2(b)).astype(BF16), st.astype(BF16), (((1,), (1,)), ((), ())),
                                preferred_element_type=F32)
        o_ref[0, 0, rows, cols] = o
        kd = jnp.exp2(lk + (b_end - b)).astype(BF16)
        upd = lax.dot_general(v, kd, (((0,), (0,)), ((), ())), preferred_element_type=F32)
        st_ref[0, 0, hd] = st * jnp.exp2(b_end) + upd


def _hgrn_body(q_ref, z_ref, v_ref, lb_ref, tri_ref, sel_ref, s0_ref, o_ref, st_ref, kbuf, bbuf, pcat):
    d, j = pl.program_id(1), pl.program_id(2)
    n_chunks = q_ref.shape[1] // HGRN_CHUNK

    @pl.when(j == 0)
    def _():
        st_ref[...] = s0_ref[...]

    for rev in (False, True):
        @pl.when(d == int(rev))
        def _(rev=rev):
            for i in range(n_chunks):
                ci = (n_chunks - 1 - i) if rev else i
                _hgrn_chunk(ci * HGRN_CHUNK, rev, q_ref, z_ref, v_ref, lb_ref, tri_ref, sel_ref, o_ref, st_ref,
                            kbuf.at[ci], bbuf.at[ci], pcat.at[ci])


def _hgrn(q, z, v, lbtab, tri, sel, s0):
    b, n, _ = q.shape
    tb = min(HGRN_BLOCK, n)
    nblk = n // tb
    c = HGRN_CHUNK

    def blk(d, j):
        return jnp.where(d == 0, j, nblk - 1 - j)

    tok = lambda i, d, j: (i, blk(d, j), 0)
    st_spec = pl.BlockSpec((1, 1, B_HEADS, LANES, LANES), lambda i, d, j: (i, d, 0, 0, 0))
    return pl.pallas_call(
        _hgrn_body,
        grid=(b, 2, nblk),
        in_specs=[
            pl.BlockSpec((1, tb, B_W), tok),
            pl.BlockSpec((1, tb, B_W), lambda i, d, j: (i, blk(d, j), d)),
            pl.BlockSpec((1, tb, B_W), tok),
            pl.BlockSpec((1, 8, B_W), lambda i, d, j: (d, 0, 0)),
            pl.BlockSpec((None, c, 3 * c), lambda i, d, j: (d, 0, 0)),
            _const_spec(sel.shape),
            st_spec,
        ],
        out_specs=[
            pl.BlockSpec((1, 1, tb, B_W), lambda i, d, j: (d, i, blk(d, j), 0)),
            st_spec,
        ],
        out_shape=[jax.ShapeDtypeStruct((2, b, n, B_W), F32),
                   jax.ShapeDtypeStruct(s0.shape, F32)],
        scratch_shapes=[pltpu.VMEM((tb // c, B_HEADS, c, LANES), F32),
                        pltpu.VMEM((tb // c, B_HEADS, c, LANES), F32),
                        pltpu.VMEM((tb // c, B_HEADS * c, HGRN_SUB * LANES), BF16)],
        compiler_params=_cparams("parallel", "parallel", "arbitrary"),
        name="hgrn_scan",
    )(q, z, v, lbtab, tri, sel, s0)


def _odd_in_body(x_ref, mod_ref, nw_ref, w_ref, q_ref, k_ref, v_ref):
    h = _norm_mod(x_ref[...], nw_ref[...], mod_ref[0, 1:2, :], mod_ref[0, 0:1, :]).astype(BF16)
    cw = q_ref.shape[1]
    q_ref[...] = (jnp.dot(h, w_ref[:, 0:cw], preferred_element_type=F32) * ATTN_SCALE).astype(BF16)
    k_ref[...] = jnp.dot(h, w_ref[:, cw:2 * cw], preferred_element_type=F32).astype(BF16)
    v_ref[...] = jnp.dot(h, w_ref[:, 2 * cw:3 * cw], preferred_element_type=F32).astype(BF16)


def _odd_in(x, modtab, norm_w, w_qkv, *, seq_len, mod_row, tm):
    n, d = x.shape
    cw = w_qkv[0].shape[2] // 3
    row = pl.BlockSpec((tm, cw), lambda i: (i, 0))
    return pl.pallas_call(
        _odd_in_body,
        grid=(n // tm,),
        in_specs=[
            pl.BlockSpec((tm, d), lambda i: (i, 0)),
            pl.BlockSpec((1, 6, d), _mod_index(seq_len, tm, mod_row)),
            _const_spec((1, d)),
            _layer_spec(*w_qkv),
        ],
        out_specs=[row, row, row],
        out_shape=[jax.ShapeDtypeStruct((n, cw), BF16)] * 3,
        compiler_params=_cparams("parallel"),
        name="odd_in",
    )(x, modtab, norm_w.reshape(1, d), w_qkv[0])


def _na_body(ql_ref, kl_ref, vl_ref, qc_ref, kc_ref, vc_ref, bias_ref, ol_ref, oc_ref, *, rows):
    band = NA_ROWS * GRID_W
    lane_q = lax.broadcasted_iota(jnp.int32, (GRID_W, LANES), 1) < HEAD_DIM
    kc, vc = kc_ref[0], vc_ref[0]
    nt = (((1,), (1,)), ((), ()))

    def two_heads(q, lo_mask):
        zero = jnp.zeros_like(q)
        return jnp.concatenate([jnp.where(lo_mask, q, zero), jnp.where(lo_mask, zero, q)], axis=0)

    def merge(o, n, lo_mask):
        return jnp.where(lo_mask, o[:n], o[n:])

    def with_ones(v):
        return jnp.concatenate([v, jnp.ones_like(v)], axis=1)

    vc_aug = with_ones(vc)

    def row_group(i, carry):
        n2 = 2 * GRID_W
        rs = [i * NA_ROW_UNROLL + u for u in range(NA_ROW_UNROLL)]
        q = jnp.concatenate(
            [two_heads(ql_ref[0, pl.ds(pl.multiple_of(r * GRID_W, GRID_W), GRID_W), :], lane_q) for r in rs], axis=0)
        s_ctx = lax.dot_general(q, kc, nt, preferred_element_type=F32)
        p_ctx, p_loc, v_band = [], [], []
        for u, r in enumerate(rs):
            r0 = jnp.clip(r - NA_ROWS // 2, 0, rows - NA_ROWS)
            kstart = pl.multiple_of(r0 * GRID_W, GRID_W)
            s_loc = lax.dot_general(q[u * n2:(u + 1) * n2], kl_ref[0, pl.ds(kstart, band), :], nt,
                                    preferred_element_type=F32) + bias_ref[0, r - r0]
            s_c = s_ctx[u * n2:(u + 1) * n2]
            m = jnp.maximum(jnp.max(s_loc, axis=1, keepdims=True), jnp.max(s_c, axis=1, keepdims=True))
            p_loc.append(jnp.exp(s_loc - m).astype(BF16))
            p_ctx.append(jnp.exp(s_c - m).astype(BF16))
            v_band.append(vl_ref[0, pl.ds(kstart, band), :])
        o_ctx = jnp.dot(jnp.concatenate(p_ctx, axis=0), vc_aug, preferred_element_type=F32)
        for u, r in enumerate(rs):
            o = jnp.dot(p_loc[u], with_ones(v_band[u]), preferred_element_type=F32) + o_ctx[u * n2:(u + 1) * n2]
            o = o[:, :LANES] / o[:, LANES:]
            ol_ref[0, pl.ds(pl.multiple_of(r * GRID_W, GRID_W), GRID_W), :] = (
                merge(o, GRID_W, lane_q).astype(BF16))
        return carry

    lax.fori_loop(0, rows // NA_ROW_UNROLL, row_group, 0)

    n_ctx = qc_ref.shape[1]
    lane_c = lax.broadcasted_iota(jnp.int32, (n_ctx, LANES), 1) < HEAD_DIM
    q = two_heads(qc_ref[0], lane_c)
    s = lax.dot_general(q, kc, nt, preferred_element_type=F32)
    p = jnp.exp(s - jnp.max(s, axis=1, keepdims=True)).astype(BF16)
    o = jnp.dot(p, vc_aug, preferred_element_type=F32)
    oc_ref[0] = merge(o[:, :LANES] / o[:, LANES:], n_ctx, lane_c).astype(BF16)


def _na_bias(rpb):
    heads, n_dy, n_dx = rpb.shape
    qc = np.arange(GRID_W)
    kc = np.arange(GRID_W)
    c_start = np.clip(qc - NA_COLS // 2, 0, GRID_W - NA_COLS)
    in_win = (kc[None, :] >= c_start[:, None]) & (kc[None, :] < c_start[:, None] + NA_COLS)
    dx = kc[None, :] - qc[:, None] + (NA_COLS - 1)
    onehot = ((dx[:, :, None] == np.arange(n_dx)[None, None, :]) & in_win[:, :, None]).astype(np.float32)
    t = jnp.einsum("hyx,qcx->hqyc", rpb.astype(F32), jnp.asarray(onehot), precision=HIGHEST)
    t = jnp.where(jnp.asarray(in_win)[None, :, None, :], t, NEG_BIG).reshape(heads, GRID_W, n_dy * GRID_W)
    width = NA_ROWS * GRID_W
    tab = jnp.stack([t[:, :, (NA_ROWS - 1 - var) * GRID_W:(NA_ROWS - 1 - var) * GRID_W + width]
                     for var in range(NA_ROWS)], axis=1)
    tab = tab.reshape(heads // 2, 2, NA_ROWS, GRID_W, width)
    return tab.transpose(0, 2, 1, 3, 4).reshape(heads // 2, NA_ROWS, 2 * GRID_W, width)


def _na(q_lat, k_lat, v_lat, q_ctx, k_ctx, v_ctx, bias):
    b, n, w = q_lat.shape
    n_ctx = q_ctx.shape[1]
    rows = n // GRID_W
    assert rows >= NA_ROWS and rows % NA_ROW_UNROLL == 0
    pairs = w // LANES
    lat = pl.BlockSpec((1, n, LANES), lambda p, i: (i, 0, p))
    ctx = pl.BlockSpec((1, n_ctx, LANES), lambda p, i: (i, 0, p))
    return pl.pallas_call(
        functools.partial(_na_body, rows=rows),
        grid=(pairs, b),
        in_specs=[lat, lat, lat, ctx, ctx, ctx,
                  pl.BlockSpec((1,) + bias.shape[1:], lambda p, i: (p, 0, 0, 0))],
        out_specs=[lat, ctx],
        out_shape=[jax.ShapeDtypeStruct((b, n, w), BF16), jax.ShapeDtypeStruct((b, n_ctx, w), BF16)],
        compiler_params=_cparams("parallel", "parallel"),
        name="na_attn",
    )(q_lat, k_lat, v_lat, q_ctx, k_ctx, v_ctx, bias)


def _post(acc, post_w, gate, x):
    return x + gate * _rms(acc, post_w)


def _even_out_body(x_ref, mod_ref, oa_ref, ob_ref, sg_ref, onw_ref, w_ref, pw_ref, y_ref):
    acc = jnp.dot(oa_ref[...], w_ref[0:A_Q_W, :], preferred_element_type=F32)
    ob = ob_ref[0] + ob_ref[1]
    parts = []
    for hd in range(B_HEADS):
        cols = slice(hd * LANES, (hd + 1) * LANES)
        parts.append((_rms(ob[:, cols], onw_ref[...]) * sg_ref[:, cols].astype(F32)).astype(BF16))
    acc = acc + jnp.dot(jnp.concatenate(parts, axis=1), w_ref[A_Q_W:, :], preferred_element_type=F32)
    y_ref[...] = _post(acc, pw_ref[...], mod_ref[0, 2:3, :], x_ref[...])


def _even_out(x, modtab, oa, ob, sg, out_norm, w_out, post_w, *, seq_len, mod_row, tm):
    n, d = x.shape
    row = lambda w: pl.BlockSpec((tm, w), lambda i: (i, 0))
    return pl.pallas_call(
        _even_out_body,
        grid=(n // tm,),
        in_specs=[
            row(d),
            pl.BlockSpec((1, 6, d), _mod_index(seq_len, tm, mod_row)),
            row(A_Q_W),
            pl.BlockSpec((2, tm, B_W), lambda i: (0, i, 0)),
            row(B_W),
            _const_spec((1, LANES)),
            _layer_spec(*w_out),
            _const_spec((1, d)),
        ],
        out_specs=row(d),
        out_shape=jax.ShapeDtypeStruct((n, d), F32),
        compiler_params=_cparams("parallel"),
        name="even_out",
    )(x, modtab, oa, ob, sg, out_norm.reshape(1, LANES), w_out[0], post_w.reshape(1, d))


def _odd_out_body(x_ref, mod_ref, o_ref, w_ref, pw_ref, y_ref):
    acc = jnp.dot(o_ref[...], w_ref[...], preferred_element_type=F32)
    y_ref[...] = _post(acc, pw_ref[...], mod_ref[0, 2:3, :], x_ref[...])


def _odd_out(x, modtab, o, w_out, post_w, *, seq_len, mod_row, tm):
    n, d = x.shape
    row = lambda w: pl.BlockSpec((tm, w), lambda i: (i, 0))
    return pl.pallas_call(
        _odd_out_body,
        grid=(n // tm,),
        in_specs=[
            row(d),
            pl.BlockSpec((1, 6, d), _mod_index(seq_len, tm, mod_row)),
            row(o.shape[1]),
            _layer_spec(*w_out),
            _const_spec((1, d)),
        ],
        out_specs=row(d),
        out_shape=jax.ShapeDtypeStruct((n, d), F32),
        compiler_params=_cparams("parallel"),
        name="odd_out",
    )(x, modtab, o, w_out[0], post_w.reshape(1, d))


FFN_HALO = BF16_SUBLANES
FFN_TN = 256
FFN_TM = 1024


def _ffn_body(x_ref, xp_ref, xn_ref, mod_ref, nw_ref, wup_ref, cw_ref, cb_ref, wdn_ref, pw_ref, y_ref,
              hbuf, ua, ug, act, *, seq_len):
    tm, d = x_ref.shape
    dff = wdn_ref.shape[0]
    halo = FFN_HALO
    scale, shift = mod_ref[0, 4:5, :], mod_ref[0, 3:4, :]
    x = x_ref[...]
    row0 = pl.program_id(0) * tm
    top_in = (row0 % seq_len) != 0
    bottom_in = ((row0 + tm) % seq_len) != 0
    hbuf[0:halo, :] = jnp.where(top_in, _norm_mod(xp_ref[...], nw_ref[...], scale, shift), 0.0).astype(BF16)
    hbuf[halo:halo + tm, :] = _norm_mod(x, nw_ref[...], scale, shift).astype(BF16)
    hbuf[halo + tm:, :] = jnp.where(bottom_in, _norm_mod(xn_ref[...], nw_ref[...], scale, shift), 0.0).astype(BF16)
    h = hbuf[...]
    for j in range(dff // FFN_TN):
        halves = []
        for buf, base in ((ua, 0), (ug, dff)):
            cols = slice(base + j * FFN_TN, base + (j + 1) * FFN_TN)
            buf[...] = jnp.dot(h, wup_ref[:, cols], preferred_element_type=F32)
            halves.append(
                cw_ref[0:1, cols] * buf[halo - 1:halo - 1 + tm, :]
                + cw_ref[1:2, cols] * buf[halo:halo + tm, :]
                + cw_ref[2:3, cols] * buf[halo + 1:halo + 1 + tm, :]
                + cb_ref[0:1, cols])
        a, g = halves
        act[:, j * FFN_TN:(j + 1) * FFN_TN] = (_silu(g) * a).astype(BF16)
    out = jnp.dot(act[...], wdn_ref[...], preferred_element_type=F32)
    y_ref[...] = _post(out, pw_ref[...], mod_ref[0, 5:6, :], x)


def _ffn(x, modtab, norm_w, w_up, conv_w, conv_b, w_down, post_w, *, seq_len, mod_row, tm):
    n, d = x.shape
    dff = w_down[0].shape[1]
    assert dff % FFN_TN == 0 and tm % FFN_HALO == 0 and seq_len % tm == 0
    per_tile = tm // FFN_HALO
    last = n // FFN_HALO - 1
    return pl.pallas_call(
        functools.partial(_ffn_body, seq_len=seq_len),
        grid=(n // tm,),
        in_specs=[
            pl.BlockSpec((tm, d), lambda i: (i, 0)),
            pl.BlockSpec((FFN_HALO, d), lambda i: (jnp.maximum(i * per_tile - 1, 0), 0)),
            pl.BlockSpec((FFN_HALO, d), lambda i: (jnp.minimum((i + 1) * per_tile, last), 0)),
            pl.BlockSpec((1, 6, d), _mod_index(seq_len, tm, mod_row)),
            _const_spec((1, d)),
            _layer_spec(*w_up),
            _const_spec(conv_w.shape),
            _const_spec((1, 2 * dff)),
            _layer_spec(*w_down),
            _const_spec((1, d)),
        ],
        out_specs=pl.BlockSpec((tm, d), lambda i: (i, 0)),
        out_shape=jax.ShapeDtypeStruct((n, d), F32),
        scratch_shapes=[pltpu.VMEM((tm + 2 * FFN_HALO, d), BF16),
                        pltpu.VMEM((tm + 2 * FFN_HALO, FFN_TN), F32),
                        pltpu.VMEM((tm + 2 * FFN_HALO, FFN_TN), F32),
                        pltpu.VMEM((tm, dff), BF16)],
        compiler_params=_cparams("parallel"),
        name="conv_ffn",
    )(x, x, x, modtab, norm_w.reshape(1, d), w_up[0], conv_w, conv_b.reshape(1, 2 * dff), w_down[0],
      post_w.reshape(1, d))


def _rope_tables(n_tokens):
    t = jnp.arange(n_tokens)
    row = (t // GRID_W).astype(F32)
    col = (t % GRID_W).astype(F32)
    n_freq = HEAD_DIM // 4
    inv_freq = ROPE_THETA ** (-jnp.arange(n_freq, dtype=F32) / n_freq)
    ang = jnp.concatenate([row[:, None] * inv_freq, col[:, None] * inv_freq], axis=-1)
    cos, sin = jnp.cos(ang), jnp.sin(ang)
    cos_t = jnp.tile(cos, (1, LANES // (HEAD_DIM // 2)))
    sin_t = jnp.tile(jnp.concatenate([-sin, sin], axis=-1), (1, LANES // HEAD_DIM))
    return cos_t, sin_t


def _hgrn_tables():
    c = HGRN_CHUNK
    lower = jnp.tril(jnp.ones((c, c), F32))
    tri = jnp.stack([lower, lower.T])
    tri = jnp.concatenate([tri] * 3, axis=2).astype(BF16)
    u = jnp.repeat(jnp.arange(HGRN_SUB), LANES)
    sel = (u[:, None] == (jnp.arange(c) % HGRN_SUB)[None, :]).astype(BF16)
    return tri, sel


def _lb_table(lb):
    pad = jnp.zeros((2, 6, lb.shape[-1]), F32)
    return jnp.concatenate([jnp.log(lb)[:, None], jnp.log1p(-lb)[:, None], pad], axis=1)


def kernel(x, c, ctx, c_ctx, w_mod, b_mod, norm_pre_mix, norm_post_mix, norm_pre_ffn, norm_post_ffn, even_w_in, even_w_out, even_q_norm, even_k_norm, hgrn_lb_logits, hgrn_out_norm, odd_w_qkv, odd_w_out, odd_rpb, ffn_w_up, ffn_conv_w, ffn_conv_b, ffn_w_down):
    bsz, seq, d = x.shape
    n_ctx = ctx.shape[1]
    depth = w_mod.shape[0]
    n_even = even_w_in.shape[0]

    modtab = _modulation(c, c_ctx, w_mod, b_mod)
    even_w_in, even_w_out, odd_w_qkv, odd_w_out, ffn_w_up, ffn_w_down = (
        _to_bf16(w) for w in (even_w_in, even_w_out, odd_w_qkv, odd_w_out, ffn_w_up, ffn_w_down))
    ctx_row = bsz
    rope = _rope_tables(seq)
    gmat = jnp.kron(jnp.eye(LANES // HEAD_DIM, dtype=F32), jnp.full((HEAD_DIM, HEAD_DIM), 1.0 / HEAD_DIM, F32))
    gmat = jnp.concatenate([gmat, gmat], axis=0).astype(BF16)
    tri, sel = _hgrn_tables()
    p_lb = jax.nn.softmax(hgrn_lb_logits.astype(F32), axis=0)
    lb_all = jnp.concatenate([jnp.zeros_like(p_lb[:1]), jnp.cumsum(p_lb[1:], axis=0)], axis=0)

    xl = x.reshape(bsz * seq, d)
    xc = ctx.reshape(bsz * n_ctx, d)
    tm_lat = min(512, seq)
    tm_ctx = min(512, bsz * n_ctx)
    lat_kw = dict(seq_len=seq, mod_row=None, tm=tm_lat)
    ctx_kw = dict(seq_len=n_ctx, mod_row=ctx_row, tm=tm_ctx)

    for l in range(depth):
        with_ctx_out = l < depth - 1
        mt = modtab[l]
        if l % 2 == 0:
            e = l // 2
            w_in = (even_w_in, e)
            w_out = (even_w_out, e)
            qn = jnp.tile(even_q_norm[e], LANES // HEAD_DIM).reshape(1, LANES)
            kn = jnp.tile(even_k_norm[e], LANES // HEAD_DIM).reshape(1, LANES)
            pl_ = _even_in(xl, mt, norm_pre_mix[l], w_in, qn, kn, rope, gmat, **lat_kw)
            pc_ = _even_in(xc, mt, norm_pre_mix[l], w_in, qn, kn, None, gmat, **ctx_kw)
            qa_l, ka_l, va_l, bq_l, z_l, bi_l, sg_l = pl_
            qa_c, ka_c, va_c, bq_c, z_c, bi_c, sg_c = pc_
            r3 = lambda a, n: a.reshape(bsz, n, a.shape[-1])
            oa_l = _gqa(r3(qa_l, seq), r3(ka_c, n_ctx), r3(va_c, n_ctx), r3(ka_l, seq), r3(va_l, seq),
                        tq=min(GQA_TQ, seq))
            lbtab = _lb_table(lb_all[e])
            s0 = jnp.zeros((bsz, 2, B_HEADS, LANES, LANES), F32)
            ob_c, s_ctx = _hgrn(r3(bq_c, n_ctx), r3(z_c, n_ctx), r3(bi_c, n_ctx), lbtab, tri, sel, s0)
            ob_l, _ = _hgrn(r3(bq_l, seq), r3(z_l, seq), r3(bi_l, seq), lbtab, tri, sel, s_ctx)
            xl_new = _even_out(xl, mt, oa_l.reshape(bsz * seq, A_Q_W), ob_l.reshape(2, bsz * seq, B_W), sg_l,
                               hgrn_out_norm[e], w_out, norm_post_mix[l], **lat_kw)
            if with_ctx_out:
                oa_c = _gqa(r3(qa_c, n_ctx), r3(ka_c, n_ctx), r3(va_c, n_ctx), tq=n_ctx)
                xc = _even_out(xc, mt, oa_c.reshape(bsz * n_ctx, A_Q_W), ob_c.reshape(2, bsz * n_ctx, B_W), sg_c,
                               hgrn_out_norm[e], w_out, norm_post_mix[l], **ctx_kw)
            xl = xl_new
        else:
            o = l // 2
            w_qkv = (odd_w_qkv, o)
            w_out = (odd_w_out, o)
            ql, kl, vl = _odd_in(xl, mt, norm_pre_mix[l], w_qkv, **lat_kw)
            qc, kc, vc = _odd_in(xc, mt, norm_pre_mix[l], w_qkv, **ctx_kw)
            r3 = lambda a, n: a.reshape(bsz, n, a.shape[-1])
            o_l, o_c = _na(r3(ql, seq), r3(kl, seq), r3(vl, seq), r3(qc, n_ctx), r3(kc, n_ctx), r3(vc, n_ctx),
                           _na_bias(odd_rpb[o]))
            xl = _odd_out(xl, mt, o_l.reshape(bsz * seq, -1), w_out, norm_post_mix[l], **lat_kw)
            if with_ctx_out:
                xc = _odd_out(xc, mt, o_c.reshape(bsz * n_ctx, -1), w_out, norm_post_mix[l], **ctx_kw)

        w_up = (ffn_w_up, l)
        w_dn = (ffn_w_down, l)
        ffn_args = (norm_pre_ffn[l], w_up, ffn_conv_w[l], ffn_conv_b[l], w_dn, norm_post_ffn[l])
        xl = _ffn(xl, mt, *ffn_args, seq_len=seq, mod_row=None, tm=min(FFN_TM, seq))
        if with_ctx_out:
            xc = _ffn(xc, mt, *ffn_args, seq_len=n_ctx, mod_row=ctx_row, tm=n_ctx)
    return xl.reshape(bsz, seq, d)
```
